```python
import math, functools
import jax, jax.numpy as jnp
from jax import lax
import numpy as np


D_MODEL = 1024
BATCH = 8
SEQ = 4096
DEPTH = 2
DEC_BATCH = 8
DEC_SEQ = 64
PAST_LEN = 2048

CHUNK = 64
D_CONV = 1024
CONV_W = 31
N_HEADS = 16
N_KV_HEADS = 4
HEAD_DIM = 64
GROUP = N_HEADS // N_KV_HEADS
D_ATTN = N_HEADS * HEAD_DIM
IDX_HEADS = 8
IDX_DIM = 64
TOPK_MAX = 256
N_BUCKETS = 32
MAX_DIST = 128
Q_BLOCK = 128
EPS = 1e-6
SPLITS = (2 * D_CONV, D_CONV, D_ATTN, N_KV_HEADS * HEAD_DIM, N_KV_HEADS * HEAD_DIM, D_ATTN,
          IDX_HEADS * IDX_DIM, IDX_DIM, IDX_HEADS, 2 * D_MODEL)
D_IN = 2 * D_CONV + D_CONV + 2 * D_ATTN + 2 * N_KV_HEADS * HEAD_DIM + IDX_HEADS * IDX_DIM + IDX_DIM + IDX_HEADS + 2 * D_MODEL

kernel_name = 'hybrid_conformer_dsa_stream_step'


def rmsnorm(x, g):
    xf = x.astype(jnp.float32)
    y = xf * lax.rsqrt(jnp.mean(xf * xf, axis=-1, keepdims=True) + EPS)
    return (y * g.astype(jnp.float32)).astype(x.dtype)


def layernorm(x, g, b):
    xf = x.astype(jnp.float32)
    mu = jnp.mean(xf, axis=-1, keepdims=True)
    var = jnp.mean(jnp.square(xf - mu), axis=-1, keepdims=True)
    y = (xf - mu) * lax.rsqrt(var + EPS) * g.astype(jnp.float32) + b.astype(jnp.float32)
    return y.astype(x.dtype)


def split_in(z):
    offs, acc = [], 0
    for s in SPLITS[:-1]:
        acc += s
        offs.append(acc)
    return jnp.split(z, offs, axis=-1)


def t5_bucket(rel):
    nb = N_BUCKETS // 2
    ret = jnp.where(rel > 0, nb, 0)
    n = jnp.abs(rel)
    max_exact = nb // 2
    nf = jnp.maximum(n, 1).astype(jnp.float32)
    large = max_exact + (jnp.log(nf / max_exact) / math.log(MAX_DIST / max_exact) * (nb - max_exact)).astype(jnp.int32)
    large = jnp.minimum(large, nb - 1)
    return ret + jnp.where(n < max_exact, n, large)


def gather_rows(t, idx):
    return jax.vmap(lambda tb, ib: tb[ib])(t, idx)


def attend_block(q, qi, wi, qpos, k_all, v_all, ki_all, kpos, rel_bias, top_k):
    f32 = jnp.float32
    B, Q = q.shape[0], q.shape[1]
    s = jnp.einsum('bqhd,bsd->bqhs', qi.astype(f32), ki_all.astype(f32)) * (IDX_DIM ** -0.5)
    score = jnp.einsum('bqhs,bqh->bqs', jax.nn.relu(s), wi.astype(f32))
    qc = qpos // CHUNK
    admissible = (kpos[None, :] // CHUNK) <= qc[:, None]
    score = jnp.where(admissible[None], score, -jnp.inf)
    _, idx = lax.top_k(score, top_k)
    sel_pos = kpos[idx]
    valid = (sel_pos // CHUNK) <= qc[None, :, None]
    k_sel = gather_rows(k_all, idx).astype(f32)
    v_sel = gather_rows(v_all, idx).astype(f32)
    qg = q.reshape(B, Q, N_KV_HEADS, GROUP, HEAD_DIM).astype(f32)
    logits = jnp.einsum('bqhgd,bqkhd->bqhgk', qg, k_sel) * (HEAD_DIM ** -0.5)
    bias = rel_bias.astype(f32)[t5_bucket(sel_pos - qpos[None, :, None])]
    bias = bias.reshape(B, Q, top_k, N_KV_HEADS, GROUP).transpose(0, 1, 3, 4, 2)
    logits = jnp.where(valid[:, :, None, None, :], logits + bias, -jnp.inf)
    p = jax.nn.softmax(logits, axis=-1)
    o = jnp.einsum('bqhgk,bqkhd->bqhgd', p, v_sel)
    return o.reshape(B, Q, D_ATTN).astype(q.dtype)


def sparse_attention(q, k, v, qi, ki, wi, past, rel_bias):
    B, Lq = q.shape[0], q.shape[1]
    if past is None:
        P = 0
        k_all, v_all, ki_all = k, v, ki
    else:
        k_past, v_past, ki_past = past
        P = k_past.shape[1]
        k_all = jnp.concatenate([k_past.astype(k.dtype), k], axis=1)
        v_all = jnp.concatenate([v_past.astype(v.dtype), v], axis=1)
        ki_all = jnp.concatenate([ki_past.astype(ki.dtype), ki], axis=1)
    Lk = P + Lq
    top_k = min(TOPK_MAX, Lk // 4)
    qpos = P + jnp.arange(Lq, dtype=jnp.int32)
    kpos = jnp.arange(Lk, dtype=jnp.int32)
    attend = functools.partial(attend_block, k_all=k_all, v_all=v_all, ki_all=ki_all,
                               kpos=kpos, rel_bias=rel_bias, top_k=top_k)
    if Lq > Q_BLOCK:
        nb = Lq // Q_BLOCK

        def to_blocks(t):
            return jnp.moveaxis(t.reshape((B, nb, Q_BLOCK) + t.shape[2:]), 1, 0)

        out = lax.map(lambda a: attend(a[0], a[1], a[2], a[3]),
                      (to_blocks(q), to_blocks(qi), to_blocks(wi), qpos.reshape(nb, Q_BLOCK)))
        return jnp.moveaxis(out, 0, 1).reshape(B, Lq, D_ATTN)
    return attend(q, qi, wi, qpos)


def conv_branch(a, conv_prev, conv_w, conv_b, ln_g, ln_b):
    u = a[..., :D_CONV] * jax.nn.sigmoid(a[..., D_CONV:])
    up = jnp.concatenate([conv_prev.astype(u.dtype), u], axis=1)
    y = lax.conv_general_dilated(up, conv_w[:, None, :].astype(u.dtype), window_strides=(1,),
                                 padding='VALID', dimension_numbers=('NWC', 'WIO', 'NWC'),
                                 feature_group_count=D_CONV) + conv_b
    y = jax.nn.silu(layernorm(y, ln_g, ln_b))
    return y, up[:, -(CONV_W - 1):]


def sublayer(x, c, conv_prev, past, rel_bias, ada_w, ada_b, g_pre, g_post, w_in,
             conv_w, conv_b, ln_g, ln_b, w_conv_out, w_attn_out, w_out):
    B, L = x.shape[0], x.shape[1]
    shift, scale, gate = jnp.split(jax.nn.silu(c) @ ada_w + ada_b, 3, axis=-1)
    h = rmsnorm(x, g_pre) * (1 + scale[:, None]) + shift[:, None]
    z = h @ w_in
    a_conv, g_conv, q, k, v, g_attn, qi, ki, wi, gm = split_in(z)
    y_c, conv_new = conv_branch(a_conv, conv_prev, conv_w, conv_b, ln_g, ln_b)
    y_c = (y_c * jax.nn.silu(g_conv)) @ w_conv_out
    q = q.reshape(B, L, N_HEADS, HEAD_DIM)
    k = k.reshape(B, L, N_KV_HEADS, HEAD_DIM)
    v = v.reshape(B, L, N_KV_HEADS, HEAD_DIM)
    qi = qi.reshape(B, L, IDX_HEADS, IDX_DIM)
    wi = wi * (IDX_HEADS ** -0.5)
    o = sparse_attention(q, k, v, qi, ki, wi, past, rel_bias)
    y_a = (o * jax.nn.silu(g_attn)) @ w_attn_out
    gm_c, gm_a = jnp.split(jax.nn.sigmoid(gm), 2, axis=-1)
    y = (gm_c * y_c + gm_a * y_a) @ w_out
    y = rmsnorm(y, g_post)
    return x + gate[:, None] * y, conv_new, k, v, ki


def setup_inputs(seed: int = 0) -> dict:
    key = jax.random.key(seed)
    ks = jax.random.split(key, 22)

    def n(k, shape, s):
        return jax.random.normal(k, shape, jnp.float32) * s

    return {
        'x_prompt': n(ks[0], (BATCH, SEQ, D_MODEL), 1.0),
        'x_sample': n(ks[1], (DEC_BATCH, DEC_SEQ, D_MODEL), 1.0),
        'c_prompt': n(ks[2], (BATCH, D_MODEL), 1.0),
        'c_sample': n(ks[3], (DEC_BATCH, D_MODEL), 1.0),
        'cache_k': n(ks[4], (DEPTH, DEC_BATCH, PAST_LEN, N_KV_HEADS, HEAD_DIM), 1.0),
        'cache_v': n(ks[5], (DEPTH, DEC_BATCH, PAST_LEN, N_KV_HEADS, HEAD_DIM), 1.0),
        'cache_kidx': n(ks[6], (DEPTH, DEC_BATCH, PAST_LEN, IDX_DIM), 1.0),
        'state_conv': n(ks[7], (DEPTH, DEC_BATCH, CONV_W - 1, D_CONV), 0.5),
        'rel_bias': n(ks[8], (N_BUCKETS, N_HEADS), 0.5),
        'ada_w': n(ks[9], (DEPTH, D_MODEL, 3 * D_MODEL), D_MODEL ** -0.5),
        'ada_b': n(ks[10], (DEPTH, 3 * D_MODEL), 0.02),
        'norm_pre': 1.0 + n(ks[11], (DEPTH, D_MODEL), 0.05),
        'norm_post': 1.0 + n(ks[12], (DEPTH, D_MODEL), 0.05),
        'w_in': n(ks[13], (DEPTH, D_MODEL, D_IN), D_MODEL ** -0.5),
        'conv_w': n(ks[14], (DEPTH, CONV_W, D_CONV), CONV_W ** -0.5),
        'conv_b': n(ks[15], (DEPTH, D_CONV), 0.02),
        'conv_ln_g': 1.0 + n(ks[16], (DEPTH, D_CONV), 0.05),
        'conv_ln_b': n(ks[17], (DEPTH, D_CONV), 0.02),
        'w_conv_out': n(ks[18], (DEPTH, D_CONV, D_MODEL), D_CONV ** -0.5),
        'w_attn_out': n(ks[19], (DEPTH, D_ATTN, D_MODEL), D_ATTN ** -0.5),
        'w_out': n(ks[20], (DEPTH, D_MODEL, D_MODEL), D_MODEL ** -0.5),
    }


def reference(x_prompt, x_sample, c_prompt, c_sample, cache_k, cache_v, cache_kidx, state_conv,
              rel_bias, ada_w, ada_b, norm_pre, norm_post, w_in, conv_w, conv_b, conv_ln_g,
              conv_ln_b, w_conv_out, w_attn_out, w_out):
    xp, xs = x_prompt, x_sample
    kp_l, vp_l, kip_l, cp_l = [], [], [], []
    ks_l, vs_l, kis_l, cs_l = [], [], [], []
    for l in range(DEPTH):
        params = (rel_bias, ada_w[l], ada_b[l], norm_pre[l], norm_post[l], w_in[l], conv_w[l],
                  conv_b[l], conv_ln_g[l], conv_ln_b[l], w_conv_out[l], w_attn_out[l], w_out[l])
        zeros_conv = jnp.zeros((xp.shape[0], CONV_W - 1, D_CONV), xp.dtype)
        xp, cp, kp, vp, kip = sublayer(xp, c_prompt, zeros_conv, None, *params)
        xs, cs, kss, vss, kis = sublayer(xs, c_sample, state_conv[l],
                                         (cache_k[l], cache_v[l], cache_kidx[l]), *params)
        kp_l.append(kp); vp_l.append(vp); kip_l.append(kip); cp_l.append(cp)
        ks_l.append(kss); vs_l.append(vss); kis_l.append(kis); cs_l.append(cs)
    return (xp, xs,
            jnp.stack(kp_l), jnp.stack(vp_l), jnp.stack(kip_l), jnp.stack(cp_l),
            jnp.stack(ks_l), jnp.stack(vs_l), jnp.stack(kis_l), jnp.stack(cs_l))
```

```python
import functools
import math

import jax
import jax.numpy as jnp
from jax import lax
from jax.experimental import pallas as pl
from jax.experimental.pallas import tpu as pltpu

F32 = jnp.float32
BF16 = jnp.bfloat16
I32 = jnp.int32

CHUNK = 64
CONV_W = 31
N_HEADS = 16
N_KV_HEADS = 4
HEAD_DIM = 64
GROUP = N_HEADS // N_KV_HEADS
IDX_HEADS = 8
IDX_DIM = 64
TOPK_MAX = 256
N_BUCKETS = 32
MAX_DIST = 128
EPS = 1e-6

LANES = 128
KEY_BLOCK = 256
BIAS_TILE = 128
CONV_HALO = 32
VMEM_LIMIT_BYTES = 56 * 1024 * 1024
NEG_INF = float("-inf")
INT_MIN = -2 ** 31


def _sigmoid(x):
    return 1.0 / (1.0 + jnp.exp(-x))


def _silu(x):
    return x * _sigmoid(x)


def _params(*sem):
    return pltpu.CompilerParams(dimension_semantics=sem, vmem_limit_bytes=VMEM_LIMIT_BYTES)


def _ada_kernel(c_ref, w_ref, b_ref, o_ref):
    c = c_ref[...]
    o_ref[0] = jnp.dot(_silu(c), w_ref[0], preferred_element_type=F32) + b_ref[0]


def _ada(c_all, ada_w, ada_b):
    depth, d, _ = ada_w.shape
    bc = c_all.shape[0]
    return pl.pallas_call(
        _ada_kernel,
        grid=(depth, 3),
        in_specs=[
            pl.BlockSpec((bc, d), lambda l, n: (0, 0)),
            pl.BlockSpec((1, d, d), lambda l, n: (l, 0, n)),
            pl.BlockSpec((1, 1, d), lambda l, n: (l, 0, n)),
        ],
        out_specs=pl.BlockSpec((1, bc, d), lambda l, n: (l, 0, n)),
        out_shape=jax.ShapeDtypeStruct((depth, bc, 3 * d), F32),
        compiler_params=_params("arbitrary", "arbitrary"),
        name="ada",
    )(c_all, ada_w, ada_b.reshape(depth, 1, 3 * d))


def _bias_kernel(rb_ref, o_ref):
    r = lax.broadcasted_iota(I32, (BIAS_TILE, BIAS_TILE), 0)
    c = lax.broadcasted_iota(I32, (BIAS_TILE, BIAS_TILE), 1)
    nb = N_BUCKETS // 2
    max_exact = nb // 2
    for t in range(2):
        rel = (t - 1) * BIAS_TILE + c - r
        ret = jnp.where(rel > 0, nb, 0)
        n = jnp.abs(rel)
        nf = jnp.maximum(n, 1).astype(F32)
        large = max_exact + (jnp.log(nf / max_exact) / math.log(MAX_DIST / max_exact)
                             * (nb - max_exact)).astype(I32)
        large = jnp.minimum(large, nb - 1)
        bucket = ret + jnp.where(n < max_exact, n, large)
        for h in range(N_HEADS):
            acc = jnp.zeros((BIAS_TILE, BIAS_TILE), F32)
            for b in range(N_BUCKETS):
                acc = jnp.where(bucket == b, rb_ref[b, h], acc)
            o_ref[h, t] = acc


def _bias_tiles(rel_bias):
    return pl.pallas_call(
        _bias_kernel,
        in_specs=[pl.BlockSpec(memory_space=pltpu.SMEM)],
        out_specs=pl.BlockSpec(memory_space=pltpu.VMEM),
        out_shape=jax.ShapeDtypeStruct((N_HEADS, 2, BIAS_TILE, BIAS_TILE), F32),
        name="bias_tiles",
    )(rel_bias)


def _seg_offsets(d_conv, d_attn):
    widths = (2 * d_conv, d_conv, d_attn, N_KV_HEADS * HEAD_DIM, N_KV_HEADS * HEAD_DIM, d_attn,
              IDX_HEADS * IDX_DIM, LANES, None)
    offs, acc = [], 0
    for w in widths:
        offs.append(acc)
        if w is not None:
            acc += w
    return offs


def _inproj_kernel(x_ref, mod_ref, g_ref, w_ref,
                   u_ref, sgc_ref, q_ref, k_ref, v_ref, kb_ref, vb_ref, sga_ref, qi_ref,
                   ki_ref, kib_ref, wi_ref, gm_ref, *, d_conv, d_attn, d_model):
    x = x_ref[0]
    ms = jnp.mean(x * x, axis=-1, keepdims=True)
    y = x * lax.rsqrt(ms + EPS) * g_ref[...]
    shift = mod_ref[0, 0:1, :]
    scale = mod_ref[0, 1:2, :]
    h = (y * (1.0 + scale) + shift).astype(BF16)
    o_a, o_gc, o_q, o_k, o_v, o_ga, o_qi, o_kw, o_gm = _seg_offsets(d_conv, d_attn)

    def mm(lo, width):
        return jnp.dot(h, w_ref[:, lo:lo + width], preferred_element_type=F32)

    a = mm(o_a, 2 * d_conv)
    u_ref[0] = a[:, :d_conv] * _sigmoid(a[:, d_conv:])
    sgc_ref[0] = _silu(mm(o_gc, d_conv)).astype(BF16)
    zq = mm(o_q, d_attn) * (HEAD_DIM ** -0.5)
    for hh in range(N_HEADS):
        q_ref[0, hh] = zq[:, hh * HEAD_DIM:(hh + 1) * HEAD_DIM].astype(BF16)
    zk = mm(o_k, N_KV_HEADS * HEAD_DIM)
    zv = mm(o_v, N_KV_HEADS * HEAD_DIM)
    k_ref[0] = zk
    v_ref[0] = zv
    for g in range(N_KV_HEADS):
        kb_ref[0, g] = zk[:, g * HEAD_DIM:(g + 1) * HEAD_DIM].astype(BF16)
        vb_ref[0, g] = zv[:, g * HEAD_DIM:(g + 1) * HEAD_DIM].astype(BF16)
    sga_ref[0] = _silu(mm(o_ga, d_attn)).astype(BF16)
    zqi = mm(o_qi, IDX_HEADS * IDX_DIM) * (IDX_DIM ** -0.5)
    for hh in range(IDX_HEADS):
        qi_ref[0, hh] = zqi[:, hh * IDX_DIM:(hh + 1) * IDX_DIM].astype(BF16)
    zkw = mm(o_kw, LANES)
    zki = zkw[:, :IDX_DIM]
    ki_ref[0] = zki
    kib_ref[0] = zki.astype(BF16)
    wi_ref[0] = zkw[:, IDX_DIM:IDX_DIM + IDX_HEADS] * (IDX_HEADS ** -0.5)
    gm_ref[0] = _sigmoid(mm(o_gm, 2 * d_model)).astype(BF16)


def _inproj(x, mod3, g_pre, w_pad, *, d_conv, d_attn):
    b, l, d = x.shape
    tm = min(256, l)
    nw = w_pad.shape[1]
    kvd = N_KV_HEADS * HEAD_DIM

    def row(width):
        return pl.BlockSpec((1, tm, width), lambda bi, i: (bi, i, 0))

    def heads(n, width):
        return pl.BlockSpec((1, n, tm, width), lambda bi, i: (bi, 0, i, 0))

    out_shape = (
        jax.ShapeDtypeStruct((b, l, d_conv), F32),
        jax.ShapeDtypeStruct((b, l, d_conv), BF16),
        jax.ShapeDtypeStruct((b, N_HEADS, l, HEAD_DIM), BF16),
        jax.ShapeDtypeStruct((b, l, kvd), F32),
        jax.ShapeDtypeStruct((b, l, kvd), F32),
        jax.ShapeDtypeStruct((b, N_KV_HEADS, l, HEAD_DIM), BF16),
        jax.ShapeDtypeStruct((b, N_KV_HEADS, l, HEAD_DIM), BF16),
        jax.ShapeDtypeStruct((b, l, d_attn), BF16),
        jax.ShapeDtypeStruct((b, IDX_HEADS, l, IDX_DIM), BF16),
        jax.ShapeDtypeStruct((b, l, IDX_DIM), F32),
        jax.ShapeDtypeStruct((b, l, IDX_DIM), BF16),
        jax.ShapeDtypeStruct((b, l, IDX_HEADS), F32),
        jax.ShapeDtypeStruct((b, l, 2 * d), BF16),
    )
    out_specs = (row(d_conv), row(d_conv), heads(N_HEADS, HEAD_DIM), row(kvd), row(kvd),
                 heads(N_KV_HEADS, HEAD_DIM), heads(N_KV_HEADS, HEAD_DIM), row(d_attn),
                 heads(IDX_HEADS, IDX_DIM), row(IDX_DIM), row(IDX_DIM), row(IDX_HEADS), row(2 * d))
    return pl.pallas_call(
        functools.partial(_inproj_kernel, d_conv=d_conv, d_attn=d_attn, d_model=d),
        grid=(b, l // tm),
        in_specs=[
            pl.BlockSpec((1, tm, d), lambda bi, i: (bi, i, 0)),
            pl.BlockSpec((1, 3, d), lambda bi, i: (bi, 0, 0)),
            pl.BlockSpec((1, d), lambda bi, i: (0, 0)),
            pl.BlockSpec((d, nw), lambda bi, i: (0, 0), pipeline_mode=pl.Buffered(1)),
        ],
        out_specs=out_specs,
        out_shape=out_shape,
        compiler_params=_params("arbitrary", "arbitrary"),
        name="inproj",
    )(x, mod3, g_pre, w_pad)


def _attn_kernel(q_ref, qi_ref, wi_ref, k_ref, v_ref, ki_ref, tb_ref, tri_ref, o_ref,
                 key_scr, mb_scr, *, tq, past, ksel):
    i = pl.program_id(1)
    q0 = past + i * tq
    nkb = (q0 + tq + KEY_BLOCK - 1) // KEY_BLOCK
    nt = (((1,), (1,)), ((), ()))

    qi8 = qi_ref[0].reshape(IDX_HEADS * tq, IDX_DIM)
    wi = wi_ref[0]
    qchunk = (q0 + lax.broadcasted_iota(I32, (KEY_BLOCK, tq), 1)) // CHUNK
    krow = lax.broadcasted_iota(I32, (KEY_BLOCK, tq), 0)

    def admissible(kb):
        return (kb * KEY_BLOCK + krow) // CHUNK <= qchunk

    def score_block(kb, carry):
        k0 = pl.multiple_of(kb * KEY_BLOCK, KEY_BLOCK)
        s = lax.dot_general(ki_ref[0, pl.ds(k0, KEY_BLOCK), :], qi8, nt,
                            preferred_element_type=F32)
        acc = jnp.zeros((KEY_BLOCK, tq), F32)
        for h in range(IDX_HEADS):
            acc = acc + jnp.maximum(s[:, h * tq:(h + 1) * tq], 0.0) * wi[h:h + 1, :]
        sc = jnp.where(admissible(kb), acc, NEG_INF)
        bits = pltpu.bitcast(sc, I32)
        key_scr[kb] = bits ^ ((bits >> 31) & 0x7FFFFFFF)
        return carry

    lax.fori_loop(0, nkb, score_block, 0)

    def count(pred):
        def body(kb, acc):
            hit = pred(key_scr[kb]).astype(I32)
            return acc + hit.reshape(KEY_BLOCK // 8, 8, tq).sum(axis=0)
        acc = lax.fori_loop(0, nkb, body, jnp.zeros((8, tq), I32))
        return acc.sum(axis=0, keepdims=True)

    def refine(it, thr):
        cand = thr + jnp.left_shift(jnp.int32(1), 31 - it)
        return jnp.where(count(lambda kk: kk >= cand) >= ksel, cand, thr)

    thr = lax.fori_loop(0, 32, refine, jnp.full((1, tq), INT_MIN, I32))
    need = (ksel - count(lambda kk: kk > thr)).astype(F32)

    def mask_block(kb, ties_before):
        kk = key_scr[kb]
        eq = kk == thr
        rank = jnp.dot(tri_ref[...], jnp.where(eq, 1.0, 0.0).astype(BF16),
                       preferred_element_type=F32) + ties_before
        tie = jnp.where(eq, jnp.where(rank <= need, 0.0, NEG_INF), NEG_INF)
        mb = jnp.where(admissible(kb), jnp.where(kk > thr, 0.0, tie), NEG_INF)
        mb_scr[kb] = mb.T
        return rank[KEY_BLOCK - 1:KEY_BLOCK, :]

    lax.fori_loop(0, nkb, mask_block, jnp.zeros((1, tq), F32))

    rows = GROUP * tq

    def bias_rows(h, diag):
        d0 = tb_ref[h, 1]
        dm = tb_ref[h, 0]
        far = jnp.broadcast_to(tb_ref[h, 0, 0:1, 0:1], (BIAS_TILE, BIAS_TILE))
        if diag:
            blocks = [[d0, far], [dm, d0]]
        else:
            blocks = [[far, dm], [far, far]]
        full = jnp.concatenate([jnp.concatenate(r, axis=1) for r in blocks], axis=0)
        return full[:tq]

    for g in range(N_KV_HEADS):
        q4 = q_ref[0, g * GROUP:(g + 1) * GROUP].reshape(rows, HEAD_DIM)

        def step(kb, bias, carry, g=g, q4=q4):
            m, l, acc = carry
            k0 = pl.multiple_of(kb * KEY_BLOCK, KEY_BLOCK)
            s = lax.dot_general(q4, k_ref[0, g, pl.ds(k0, KEY_BLOCK), :], nt,
                                preferred_element_type=F32)
            s = (s.reshape(GROUP, tq, KEY_BLOCK) + mb_scr[kb][None] + bias).reshape(rows, KEY_BLOCK)
            m_new = jnp.maximum(m, jnp.max(s, axis=-1, keepdims=True))
            m_safe = jnp.where(m_new == NEG_INF, 0.0, m_new)
            alpha = jnp.exp(m - m_safe)
            p = jnp.exp(s - m_safe)
            l = alpha * l + jnp.sum(p, axis=-1, keepdims=True)
            acc = alpha * acc + jnp.dot(p.astype(BF16), v_ref[0, g, pl.ds(k0, KEY_BLOCK), :],
                                        preferred_element_type=F32)
            return m_new, l, acc

        far4 = jnp.stack([tb_ref[g * GROUP + j, 0, 0:1, 0:1] for j in range(GROUP)])
        carry = (jnp.full((rows, 1), NEG_INF, F32), jnp.zeros((rows, 1), F32),
                 jnp.zeros((rows, HEAD_DIM), F32))
        carry = lax.fori_loop(0, nkb - 2, lambda kb, c: step(kb, far4, c), carry)
        off4 = jnp.stack([bias_rows(g * GROUP + j, False) for j in range(GROUP)])
        off4 = off4 + jnp.where(nkb >= 2, 0.0, NEG_INF)
        carry = step(jnp.maximum(nkb - 2, 0), off4, carry)
        diag4 = jnp.stack([bias_rows(g * GROUP + j, True) for j in range(GROUP)])
        m, l, acc = step(nkb - 1, diag4, carry)
        o = (acc / l).reshape(GROUP, tq, HEAD_DIM)
        o_ref[0, g * GROUP:(g + 1) * GROUP] = o.astype(o_ref.dtype)


def _attention(q_hm, qi_hm, wi_t, k_hm, v_hm, kib, tb, tri, *, tq, past, ksel):
    b, _, lq, _ = q_hm.shape
    lk = k_hm.shape[2]
    nkb_max = lk // KEY_BLOCK
    return pl.pallas_call(
        functools.partial(_attn_kernel, tq=tq, past=past, ksel=ksel),
        grid=(b, lq // tq),
        in_specs=[
            pl.BlockSpec((1, N_HEADS, tq, HEAD_DIM), lambda bi, i: (bi, 0, i, 0)),
            pl.BlockSpec((1, IDX_HEADS, tq, IDX_DIM), lambda bi, i: (bi, 0, i, 0)),
            pl.BlockSpec((1, IDX_HEADS, tq), lambda bi, i: (bi, 0, i)),
            pl.BlockSpec((1, N_KV_HEADS, lk, HEAD_DIM), lambda bi, i: (bi, 0, 0, 0)),
            pl.BlockSpec((1, N_KV_HEADS, lk, HEAD_DIM), lambda bi, i: (bi, 0, 0, 0)),
            pl.BlockSpec((1, lk, IDX_DIM), lambda bi, i: (bi, 0, 0)),
            pl.BlockSpec((N_HEADS, 2, BIAS_TILE, BIAS_TILE), lambda bi, i: (0, 0, 0, 0)),
            pl.BlockSpec((KEY_BLOCK, KEY_BLOCK), lambda bi, i: (0, 0)),
        ],
        out_specs=pl.BlockSpec((1, N_HEADS, tq, HEAD_DIM), lambda bi, i: (bi, 0, i, 0)),
        out_shape=jax.ShapeDtypeStruct((b, N_HEADS, lq, HEAD_DIM), BF16),
        scratch_shapes=[pltpu.VMEM((nkb_max, KEY_BLOCK, tq), I32),
                        pltpu.VMEM((nkb_max, tq, KEY_BLOCK), F32)],
        compiler_params=_params("arbitrary", "arbitrary"),
        name="attention",
    )(q_hm, qi_hm, wi_t, k_hm, v_hm, kib, tb, tri)


def _post_kernel(x_ref, mod_ref, u_ref, halo_ref, prev_ref, sgc_ref, o_ref, sga_ref, gm_ref,
                 cw_ref, cb_ref, lg_ref, lb_ref, wc_ref, wa_ref, wo_ref, gp_ref,
                 y_ref, up_scr, cv_scr, *, tt, d_conv, d_model):
    i = pl.program_id(1)
    up_scr[0:CONV_HALO] = jnp.where(i == 0, prev_ref[0], halo_ref[0])
    up_scr[CONV_HALO:] = u_ref[0]
    pad = CONV_HALO - (CONV_W - 1)
    rt = min(tt, 128)
    for r0 in range(0, tt, rt):
        for c0 in range(0, d_conv, LANES):
            acc = jnp.broadcast_to(cb_ref[:, c0:c0 + LANES], (rt, LANES))
            for j in range(CONV_W):
                acc = acc + cw_ref[j:j + 1, c0:c0 + LANES] * up_scr[pl.ds(r0 + pad + j, rt),
                                                                    c0:c0 + LANES]
            cv_scr[r0:r0 + rt, c0:c0 + LANES] = acc
    cv = cv_scr[...]
    mu = jnp.mean(cv, axis=-1, keepdims=True)
    dv = cv - mu
    var = jnp.mean(dv * dv, axis=-1, keepdims=True)
    yc = _silu(dv * lax.rsqrt(var + EPS) * lg_ref[...] + lb_ref[...])
    yc = (yc * sgc_ref[0].astype(F32)).astype(BF16)
    y_c = jnp.dot(yc, wc_ref[...], preferred_element_type=F32)
    ya = (o_ref[0].astype(F32) * sga_ref[0].astype(F32)).astype(BF16)
    y_a = jnp.dot(ya, wa_ref[...], preferred_element_type=F32)
    gm = gm_ref[0].astype(F32)
    mix = (gm[:, :d_model] * y_c + gm[:, d_model:] * y_a).astype(BF16)
    y = jnp.dot(mix, wo_ref[...], preferred_element_type=F32)
    y = y * lax.rsqrt(jnp.mean(y * y, axis=-1, keepdims=True) + EPS) * gp_ref[...]
    y_ref[0] = x_ref[0] + mod_ref[0, 2:3, :] * y


def _post(x, mod3, u, prev, sgc, o, sga, gm, conv_w, conv_b, ln_g, ln_b, wc, wa, wo, g_post):
    b, l, d = x.shape
    d_conv = u.shape[2]
    d_attn = o.shape[2]
    tt = min(256, l)
    hb = tt // CONV_HALO

    def row(width):
        return pl.BlockSpec((1, tt, width), lambda bi, i: (bi, i, 0))

    def const(shape):
        return pl.BlockSpec(shape, lambda bi, i: (0,) * len(shape))

    return pl.pallas_call(
        functools.partial(_post_kernel, tt=tt, d_conv=d_conv, d_model=d),
        grid=(b, l // tt),
        in_specs=[
            row(d),
            pl.BlockSpec((1, 3, d), lambda bi, i: (bi, 0, 0)),
            row(d_conv),
            pl.BlockSpec((1, CONV_HALO, d_conv), lambda bi, i: (bi, jnp.maximum(i * hb - 1, 0), 0)),
            pl.BlockSpec((1, CONV_HALO, d_conv), lambda bi, i: (bi, 0, 0)),
            row(d_conv), row(d_attn), row(d_attn), row(2 * d),
            const((CONV_W, d_conv)), const((1, d_conv)), const((1, d_conv)), const((1, d_conv)),
            const((d_conv, d)), const((d_attn, d)), const((d, d)), const((1, d)),
        ],
        out_specs=row(d),
        out_shape=jax.ShapeDtypeStruct((b, l, d), F32),
        scratch_shapes=[pltpu.VMEM((tt + CONV_HALO, d_conv), F32), pltpu.VMEM((tt, d_conv), F32)],
        compiler_params=_params("arbitrary", "arbitrary"),
        name="post",
    )(x, mod3, u, u, prev, sgc, o, sga, gm, conv_w, conv_b, ln_g, ln_b, wc, wa, wo, g_post)


def _pad_axis(t, axis, size):
    if t.shape[axis] == size:
        return t
    widths = [(0, 0)] * t.ndim
    widths[axis] = (0, size - t.shape[axis])
    return jnp.pad(t, widths)


def _sublayer(x, mod3, conv_prev, past, tb, tri, w):
    b, l, d = x.shape
    d_conv = w["conv_w"].shape[1]
    d_attn = N_HEADS * HEAD_DIM
    (u, sgc, q_hm, k, v, k_hm, v_hm, sga, qi_hm, ki, kib, wi, gm) = _inproj(
        x, mod3, w["g_pre"], w["w_in"], d_conv=d_conv, d_attn=d_attn)

    if past is None:
        p_len = 0
    else:
        ck, cv, cki = past
        p_len = ck.shape[1]
        k_hm = jnp.concatenate([ck.transpose(0, 2, 1, 3).astype(BF16), k_hm], axis=2)
        v_hm = jnp.concatenate([cv.transpose(0, 2, 1, 3).astype(BF16), v_hm], axis=2)
        kib = jnp.concatenate([cki.astype(BF16), kib], axis=1)
    lk = p_len + l
    ksel = min(TOPK_MAX, lk // 4)
    tq = 256 if l % 256 == 0 else 128
    lq_pad = -(-l // tq) * tq
    lk_pad = -(-(p_len + lq_pad) // KEY_BLOCK) * KEY_BLOCK
    assert p_len % KEY_BLOCK == 0 and lk % CHUNK == 0 and ksel <= KEY_BLOCK
    o_hm = _attention(
        _pad_axis(q_hm, 2, lq_pad), _pad_axis(qi_hm, 2, lq_pad),
        _pad_axis(wi.transpose(0, 2, 1), 2, lq_pad),
        _pad_axis(k_hm, 2, lk_pad), _pad_axis(v_hm, 2, lk_pad), _pad_axis(kib, 1, lk_pad),
        tb, tri, tq=tq, past=p_len, ksel=ksel)
    o = o_hm[:, :, :l].transpose(0, 2, 1, 3).reshape(b, l, d_attn)

    prev = jnp.pad(conv_prev, ((0, 0), (CONV_HALO - (CONV_W - 1), 0), (0, 0)))
    y = _post(x, mod3, u, prev, sgc, o, sga, gm, w["conv_w"], w["conv_b"], w["ln_g"], w["ln_b"],
              w["w_conv_out"], w["w_attn_out"], w["w_out"], w["g_post"])
    conv_new = jnp.concatenate([conv_prev, u], axis=1)[:, -(CONV_W - 1):]
    return (y, conv_new, k.reshape(b, l, N_KV_HEADS, HEAD_DIM), v.reshape(b, l, N_KV_HEADS, HEAD_DIM),
            ki)


def _pack_w_in(w_in, d_conv, d_attn, d_model):
    kvd = N_KV_HEADS * HEAD_DIM
    splits = (2 * d_conv, d_conv, d_attn, kvd, kvd, d_attn, IDX_HEADS * IDX_DIM, IDX_DIM,
              IDX_HEADS, 2 * d_model)
    offs = [0]
    for s in splits:
        offs.append(offs[-1] + s)
    seg = [w_in[:, offs[j]:offs[j + 1]] for j in range(len(splits))]
    kw = jnp.concatenate([seg[7], seg[8]], axis=1)
    kw = jnp.pad(kw, ((0, 0), (0, LANES - kw.shape[1])))
    return jnp.concatenate(seg[:7] + [kw, seg[9]], axis=1).astype(BF16)


def kernel(x_prompt, x_sample, c_prompt, c_sample, cache_k, cache_v, cache_kidx, state_conv,
           rel_bias, ada_w, ada_b, norm_pre, norm_post, w_in, conv_w, conv_b, conv_ln_g,
           conv_ln_b, w_conv_out, w_attn_out, w_out):
    depth = ada_w.shape[0]
    bp, _, d = x_prompt.shape
    d_conv = conv_w.shape[2]
    d_attn = N_HEADS * HEAD_DIM
    mod = _ada(jnp.concatenate([c_prompt, c_sample], axis=0), ada_w, ada_b)
    mod = mod.reshape(depth, mod.shape[1], 3, d)
    tb = _bias_tiles(rel_bias)
    tri = (jnp.arange(KEY_BLOCK)[:, None] >= jnp.arange(KEY_BLOCK)[None, :]).astype(BF16)

    xp, xs = x_prompt, x_sample
    outs = [[] for _ in range(8)]
    for l in range(depth):
        w = dict(
            g_pre=norm_pre[l][None], g_post=norm_post[l][None],
            w_in=_pack_w_in(w_in[l], d_conv, d_attn, d),
            conv_w=conv_w[l], conv_b=conv_b[l][None], ln_g=conv_ln_g[l][None],
            ln_b=conv_ln_b[l][None], w_conv_out=w_conv_out[l].astype(BF16),
            w_attn_out=w_attn_out[l].astype(BF16), w_out=w_out[l].astype(BF16))
        zeros_conv = jnp.zeros((bp, CONV_W - 1, d_conv), xp.dtype)
        xp, cp, kp, vp, kip = _sublayer(xp, mod[l, :bp], zeros_conv, None, tb, tri, w)
        xs, cs, kss, vss, kis = _sublayer(xs, mod[l, bp:], state_conv[l],
                                          (cache_k[l], cache_v[l], cache_kidx[l]), tb, tri, w)
        for lst, val in zip(outs, (kp, vp, kip, cp, kss, vss, kis, cs)):
            lst.append(val)
    return (xp, xs) + tuple(jnp.stack(o) for o in outs)
```

```python
import functools
import math

import jax
import jax.numpy as jnp
from jax import lax
from jax.experimental import pallas as pl
from jax.experimental.pallas import tpu as pltpu

F32 = jnp.float32
BF16 = jnp.bfloat16
I32 = jnp.int32
I16 = jnp.int16

CHUNK = 64
CONV_W = 31
N_HEADS = 16
N_KV_HEADS = 4
HEAD_DIM = 64
GROUP = N_HEADS // N_KV_HEADS
IDX_HEADS = 8
IDX_DIM = 64
TOPK_MAX = 256
N_BUCKETS = 32
MAX_DIST = 128
EPS = 1e-6

LANES = 128
PACKED_ROWS = 16
KEY_BLOCK = 256
BIAS_TILE = 128
CONV_HALO = 32
VMEM_LIMIT_BYTES = 56 * 1024 * 1024
NEG_INF = float("-inf")
INT_MIN = -2 ** 31
HALF_MIN = -2 ** 15
LOG2E = 1.4426950408889634


def _sigmoid(x):
    return 1.0 / (1.0 + jnp.exp(-x))


def _silu(x):
    return x * _sigmoid(x)


def _params(*sem):
    return pltpu.CompilerParams(dimension_semantics=sem, vmem_limit_bytes=VMEM_LIMIT_BYTES)


def _ada_kernel(c_ref, w_ref, b_ref, o_ref):
    c = c_ref[...]
    o_ref[0] = jnp.dot(_silu(c), w_ref[0], preferred_element_type=F32) + b_ref[0]


def _ada(c_all, ada_w, ada_b):
    depth, d, _ = ada_w.shape
    bc = c_all.shape[0]
    return pl.pallas_call(
        _ada_kernel,
        grid=(depth, 3),
        in_specs=[
            pl.BlockSpec((bc, d), lambda l, n: (0, 0)),
            pl.BlockSpec((1, d, d), lambda l, n: (l, 0, n)),
            pl.BlockSpec((1, 1, d), lambda l, n: (l, 0, n)),
        ],
        out_specs=pl.BlockSpec((1, bc, d), lambda l, n: (l, 0, n)),
        out_shape=jax.ShapeDtypeStruct((depth, bc, 3 * d), F32),
        compiler_params=_params("arbitrary", "arbitrary"),
        name="ada",
    )(c_all, ada_w, ada_b.reshape(depth, 1, 3 * d))


def _bias_kernel(rb_ref, o_ref):
    r = lax.broadcasted_iota(I32, (BIAS_TILE, BIAS_TILE), 0)
    c = lax.broadcasted_iota(I32, (BIAS_TILE, BIAS_TILE), 1)
    nb = N_BUCKETS // 2
    max_exact = nb // 2
    for t in range(2):
        rel = (t - 1) * BIAS_TILE + r - c
        ret = jnp.where(rel > 0, nb, 0)
        n = jnp.abs(rel)
        nf = jnp.maximum(n, 1).astype(F32)
        large = max_exact + (jnp.log(nf / max_exact) / math.log(MAX_DIST / max_exact)
                             * (nb - max_exact)).astype(I32)
        large = jnp.minimum(large, nb - 1)
        bucket = ret + jnp.where(n < max_exact, n, large)
        for h in range(N_HEADS):
            acc = jnp.zeros((BIAS_TILE, BIAS_TILE), F32)
            for b in range(N_BUCKETS):
                acc = jnp.where(bucket == b, rb_ref[b, h], acc)
            o_ref[h, t] = acc * LOG2E


def _bias_tiles(rel_bias):
    return pl.pallas_call(
        _bias_kernel,
        in_specs=[pl.BlockSpec(memory_space=pltpu.SMEM)],
        out_specs=pl.BlockSpec(memory_space=pltpu.VMEM),
        out_shape=jax.ShapeDtypeStruct((N_HEADS, 2, BIAS_TILE, BIAS_TILE), F32),
        name="bias_tiles",
    )(rel_bias)


def _seg_offsets(d_conv, d_attn):
    widths = (2 * d_conv, d_conv, d_attn, N_KV_HEADS * HEAD_DIM, N_KV_HEADS * HEAD_DIM, d_attn,
              IDX_HEADS * IDX_DIM, LANES, None)
    offs, acc = [], 0
    for w in widths:
        offs.append(acc)
        if w is not None:
            acc += w
    return offs


def _inproj_kernel(x_ref, mod_ref, g_ref, w_ref,
                   u_ref, sgc_ref, q_ref, k_ref, v_ref, kb_ref, vb_ref, sga_ref, qi_ref,
                   ki_ref, kib_ref, wi_ref, gm_ref, *, d_conv, d_attn, d_model):
    x = x_ref[0]
    ms = jnp.mean(x * x, axis=-1, keepdims=True)
    y = x * lax.rsqrt(ms + EPS) * g_ref[...]
    shift = mod_ref[0, 0:1, :]
    scale = mod_ref[0, 1:2, :]
    h = (y * (1.0 + scale) + shift).astype(BF16)
    o_a, o_gc, o_q, o_k, o_v, o_ga, o_qi, o_kw, o_gm = _seg_offsets(d_conv, d_attn)

    def mm(lo, width):
        return jnp.dot(h, w_ref[:, lo:lo + width], preferred_element_type=F32)

    a = mm(o_a, 2 * d_conv)
    u_ref[0] = a[:, :d_conv] * _sigmoid(a[:, d_conv:])
    sgc_ref[0] = _silu(mm(o_gc, d_conv)).astype(BF16)
    q_ref[0] = (mm(o_q, d_attn) * (HEAD_DIM ** -0.5 * LOG2E)).astype(BF16)
    zk = mm(o_k, N_KV_HEADS * HEAD_DIM)
    zv = mm(o_v, N_KV_HEADS * HEAD_DIM)
    k_ref[0] = zk
    v_ref[0] = zv
    for g in range(N_KV_HEADS):
        kb_ref[0, g] = zk[:, g * HEAD_DIM:(g + 1) * HEAD_DIM].astype(BF16)
    vb_ref[0] = zv.astype(BF16)
    sga_ref[0] = _silu(mm(o_ga, d_attn)).astype(BF16)
    qi_ref[0] = (mm(o_qi, IDX_HEADS * IDX_DIM) * (IDX_DIM ** -0.5)).astype(BF16)
    zkw = mm(o_kw, LANES)
    zki = zkw[:, :IDX_DIM]
    ki_ref[0] = zki
    kib_ref[0] = zki.astype(BF16)
    wi_ref[0] = zkw[:, IDX_DIM:IDX_DIM + IDX_HEADS] * (IDX_HEADS ** -0.5)
    gm_ref[0] = _sigmoid(mm(o_gm, 2 * d_model)).astype(BF16)


def _inproj(x, mod3, g_pre, w_pad, *, d_conv, d_attn):
    b, l, d = x.shape
    tm = min(256, l)
    nw = w_pad.shape[1]
    kvd = N_KV_HEADS * HEAD_DIM

    def row(width):
        return pl.BlockSpec((1, tm, width), lambda bi, i: (bi, i, 0))

    def heads(n, width):
        return pl.BlockSpec((1, n, tm, width), lambda bi, i: (bi, 0, i, 0))

    out_shape = (
        jax.ShapeDtypeStruct((b, l, d_conv), F32),
        jax.ShapeDtypeStruct((b, l, d_conv), BF16),
        jax.ShapeDtypeStruct((b, l, d_attn), BF16),
        jax.ShapeDtypeStruct((b, l, kvd), F32),
        jax.ShapeDtypeStruct((b, l, kvd), F32),
        jax.ShapeDtypeStruct((b, N_KV_HEADS, l, HEAD_DIM), BF16),
        jax.ShapeDtypeStruct((b, l, kvd), BF16),
        jax.ShapeDtypeStruct((b, l, d_attn), BF16),
        jax.ShapeDtypeStruct((b, l, IDX_HEADS * IDX_DIM), BF16),
        jax.ShapeDtypeStruct((b, l, IDX_DIM), F32),
        jax.ShapeDtypeStruct((b, l, IDX_DIM), BF16),
        jax.ShapeDtypeStruct((b, l, IDX_HEADS), F32),
        jax.ShapeDtypeStruct((b, l, 2 * d), BF16),
    )
    out_specs = (row(d_conv), row(d_conv), row(d_attn), row(kvd), row(kvd),
                 heads(N_KV_HEADS, HEAD_DIM), row(kvd), row(d_attn),
                 row(IDX_HEADS * IDX_DIM), row(IDX_DIM), row(IDX_DIM), row(IDX_HEADS), row(2 * d))
    return pl.pallas_call(
        functools.partial(_inproj_kernel, d_conv=d_conv, d_attn=d_attn, d_model=d),
        grid=(b, l // tm),
        in_specs=[
            pl.BlockSpec((1, tm, d), lambda bi, i: (bi, i, 0)),
            pl.BlockSpec((1, 3, d), lambda bi, i: (bi, 0, 0)),
            pl.BlockSpec((1, d), lambda bi, i: (0, 0)),
            pl.BlockSpec((d, nw), lambda bi, i: (0, 0), pipeline_mode=pl.Buffered(1)),
        ],
        out_specs=out_specs,
        out_shape=out_shape,
        compiler_params=_params("arbitrary", "arbitrary"),
        name="inproj",
    )(x, mod3, g_pre, w_pad)


def _attn_kernel(qt_ref, qit_ref, wi_ref, k_ref, vt_ref, ki_ref, tb_ref, tri_ref, o_ref,
                 key_scr, hi_scr, lo_scr, mb_scr, *, tq, past, ksel):
    i = pl.program_id(1)
    q0 = past + i * tq
    nkb = (q0 + tq + KEY_BLOCK - 1) // KEY_BLOCK

    def key_rows(kb):
        return pl.ds(pl.multiple_of(kb * KEY_BLOCK, KEY_BLOCK), KEY_BLOCK)

    qit = qit_ref[0, 0]
    wi = wi_ref[0]
    last = nkb - 1
    last_ok = (lax.broadcasted_iota(I32, (KEY_BLOCK, tq), 0) // CHUNK
               <= lax.broadcasted_iota(I32, (KEY_BLOCK, tq), 1) // CHUNK)

    def score_block(kb, is_last):
        s = jnp.dot(ki_ref[0, key_rows(kb), :], qit, preferred_element_type=F32)
        acc = jnp.zeros((KEY_BLOCK, tq), F32)
        for h in range(IDX_HEADS):
            acc = acc + jnp.maximum(s[:, h * tq:(h + 1) * tq], 0.0) * wi[h:h + 1, :]
        sc = jnp.where(last_ok, acc, NEG_INF) if is_last else acc
        bits = pltpu.bitcast(sc, I32)
        key = bits ^ ((bits >> 31) & 0x7FFFFFFF)
        key_scr[kb] = key
        hi_scr[kb] = (key >> 16).astype(I16)
        lo_scr[kb] = ((key & 0xFFFF) + HALF_MIN).astype(I16)

    def over_blocks(fn, init=0):
        carry = lax.fori_loop(0, last, lambda kb, c: fn(kb, False, c), init)
        return fn(last, True, carry)

    over_blocks(lambda kb, is_last, c: score_block(kb, is_last) or c)

    def count16(scr, pred):
        def body(kb, acc):
            hit = jnp.where(pred(scr[kb]), jnp.int16(1), jnp.int16(0))
            hit = hit.reshape(KEY_BLOCK // PACKED_ROWS, PACKED_ROWS, tq)
            parts = [hit[r] for r in range(KEY_BLOCK // PACKED_ROWS)]
            while len(parts) > 1:
                parts = [parts[r] + parts[r + 1] for r in range(0, len(parts), 2)]
            return acc + parts[0]
        acc = lax.fori_loop(0, nkb, body, jnp.zeros((PACKED_ROWS, tq), I16))
        return acc.astype(I32).sum(axis=0, keepdims=True)

    def search16(scr, target):
        def refine(it, thr):
            cand = thr + jnp.left_shift(jnp.int32(1), 15 - it)
            c16 = cand.astype(I16)
            return jnp.where(count16(scr, lambda x: x >= c16) >= target, cand, thr)
        return lax.fori_loop(0, 16, refine, jnp.full((1, tq), HALF_MIN, I32))

    thr_hi = search16(hi_scr, ksel)
    thr_hi16 = thr_hi.astype(I16)
    above = count16(hi_scr, lambda x: x > thr_hi16)

    def restrict(kb, carry):
        lo_scr[kb] = jnp.where(hi_scr[kb] == thr_hi16, lo_scr[kb], jnp.int16(HALF_MIN))
        return carry

    lax.fori_loop(0, nkb, restrict, 0)
    thr_lo = search16(lo_scr, ksel - above)
    thr = thr_hi * 65536 + (thr_lo - HALF_MIN)

    thr_lo16 = thr_lo.astype(I16)
    n_ge = above + count16(lo_scr, lambda x: x >= thr_lo16)
    surplus = jnp.max(n_ge) > ksel

    @pl.when(jnp.logical_not(surplus))
    def _():
        def mask_block(kb, is_last, carry):
            keep = jnp.where(key_scr[kb] >= thr, 0.0, NEG_INF)
            mb_scr[kb] = jnp.where(last_ok, keep, NEG_INF) if is_last else keep
            return carry

        over_blocks(mask_block)

    @pl.when(surplus)
    def _():
        def count_gt(kb, acc):
            hit = (key_scr[kb] > thr).astype(I32)
            return acc + hit.reshape(KEY_BLOCK // 8, 8, tq).sum(axis=0)

        n_gt = lax.fori_loop(0, nkb, count_gt, jnp.zeros((8, tq), I32)).sum(axis=0, keepdims=True)
        need = (ksel - n_gt).astype(F32)

        def mask_block(kb, is_last, ties_before):
            kk = key_scr[kb]
            eq = kk == thr
            rank = jnp.dot(tri_ref[...], jnp.where(eq, 1.0, 0.0).astype(BF16),
                           preferred_element_type=F32) + ties_before
            tie = jnp.where(eq, jnp.where(rank <= need, 0.0, NEG_INF), NEG_INF)
            keep = jnp.where(kk > thr, 0.0, tie)
            mb_scr[kb] = jnp.where(last_ok, keep, NEG_INF) if is_last else keep
            return rank[KEY_BLOCK - 1:KEY_BLOCK, :]

        over_blocks(mask_block, jnp.zeros((1, tq), F32))

    cols = GROUP * tq
    nsub = tq // BIAS_TILE

    def bias_block(h, diag):
        same = tb_ref[h, 1]
        prev = tb_ref[h, 0]
        far = jnp.broadcast_to(tb_ref[h, 0, 0:1, 0:1], (BIAS_TILE, BIAS_TILE))
        if diag:
            grid = [[same, prev], [far, same]]
        else:
            grid = [[far, far], [prev, far]]
        return jnp.concatenate([jnp.concatenate(r[:nsub], axis=1) for r in grid], axis=0)

    def far_row(g):
        return jnp.concatenate(
            [jnp.broadcast_to(tb_ref[g * GROUP + j, 0, 0:1, 0:1], (1, tq)) for j in range(GROUP)],
            axis=1)

    def step(kb, kind, carry):
        mb = mb_scr[kb]
        out = []
        scores = [jnp.dot(k_ref[0, g, key_rows(kb), :], qt_ref[0, g, 0], preferred_element_type=F32)
                  for g in range(N_KV_HEADS)]
        for g in range(N_KV_HEADS):
            m, acc = carry[g]
            s = scores[g]
            if kind == "far":
                shift = far_row(g)
                t = jnp.concatenate([s[:, j * tq:(j + 1) * tq] + mb for j in range(GROUP)], axis=1)
            else:
                shift = jnp.zeros((1, cols), F32)
                gate = jnp.where(nkb >= 2, 0.0, NEG_INF) if kind == "off" else 0.0
                t = jnp.concatenate(
                    [s[:, j * tq:(j + 1) * tq] + mb + (bias_block(g * GROUP + j, kind == "diag") + gate)
                     for j in range(GROUP)], axis=1)
            m_new = jnp.maximum(m, jnp.max(t, axis=0, keepdims=True) + shift)
            m_safe = jnp.where(m_new == NEG_INF, 0.0, m_new)
            p = jnp.exp2(t - (m_safe - shift))
            acc = jnp.exp2(m - m_safe) * acc + jnp.dot(vt_ref[0, g, kb], p.astype(BF16),
                                                       preferred_element_type=F32)
            out.append((m_new, acc))
        return tuple(out)

    vrows = vt_ref.shape[3]
    carry = tuple((jnp.full((1, cols), NEG_INF, F32), jnp.zeros((vrows, cols), F32))
                  for _ in range(N_KV_HEADS))
    carry = lax.fori_loop(0, nkb - 2, lambda kb, c: step(kb, "far", c), carry)
    carry = step(jnp.maximum(nkb - 2, 0), "off", carry)
    carry = step(nkb - 1, "diag", carry)
    for g in range(N_KV_HEADS):
        acc = carry[g][1]
        o = acc[:HEAD_DIM] / acc[HEAD_DIM:HEAD_DIM + 1]
        for j in range(GROUP):
            o_ref[0, g * GROUP + j] = o[:, j * tq:(j + 1) * tq].astype(o_ref.dtype)


def _attention(qt, qit, wi_t, k_hm, vt, kib, tb, tri, *, tq, past, ksel):
    b, _, nq, _, _ = qt.shape
    lk = k_hm.shape[2]
    nkb_max = lk // KEY_BLOCK
    blocks = (nkb_max, KEY_BLOCK, tq)
    return pl.pallas_call(
        functools.partial(_attn_kernel, tq=tq, past=past, ksel=ksel),
        grid=(b, nq),
        in_specs=[
            pl.BlockSpec((1, N_KV_HEADS, 1, HEAD_DIM, GROUP * tq), lambda bi, i: (bi, 0, i, 0, 0)),
            pl.BlockSpec((1, 1, IDX_DIM, IDX_HEADS * tq), lambda bi, i: (bi, i, 0, 0)),
            pl.BlockSpec((1, IDX_HEADS, tq), lambda bi, i: (bi, 0, i)),
            pl.BlockSpec((1, N_KV_HEADS, lk, HEAD_DIM), lambda bi, i: (bi, 0, 0, 0)),
            pl.BlockSpec((1, N_KV_HEADS, nkb_max, vt.shape[3], KEY_BLOCK),
                         lambda bi, i: (bi, 0, 0, 0, 0)),
            pl.BlockSpec((1, lk, IDX_DIM), lambda bi, i: (bi, 0, 0)),
            pl.BlockSpec((N_HEADS, 2, BIAS_TILE, BIAS_TILE), lambda bi, i: (0, 0, 0, 0)),
            pl.BlockSpec((KEY_BLOCK, KEY_BLOCK), lambda bi, i: (0, 0)),
        ],
        out_specs=pl.BlockSpec((1, N_HEADS, HEAD_DIM, tq), lambda bi, i: (bi, 0, 0, i)),
        out_shape=jax.ShapeDtypeStruct((b, N_HEADS, HEAD_DIM, nq * tq), BF16),
        scratch_shapes=[pltpu.VMEM(blocks, I32), pltpu.VMEM(blocks, I16), pltpu.VMEM(blocks, I16),
                        pltpu.VMEM(blocks, F32)],
        compiler_params=_params("arbitrary", "arbitrary"),
        name="attention",
    )(qt, qit, wi_t, k_hm, vt, kib, tb, tri)


def _post_kernel(x_ref, mod_ref, u_ref, halo_ref, prev_ref, sgc_ref, o_ref, sga_ref, gm_ref,
                 cw_ref, cb_ref, lg_ref, lb_ref, wc_ref, wa_ref, wo_ref, gp_ref,
                 y_ref, up_scr, cv_scr, *, tt, d_conv, d_model):
    i = pl.program_id(1)
    up_scr[0:CONV_HALO] = jnp.where(i == 0, prev_ref[0], halo_ref[0])
    up_scr[CONV_HALO:] = u_ref[0]
    pad = CONV_HALO - (CONV_W - 1)
    rt = min(tt, 128)
    for r0 in range(0, tt, rt):
        for c0 in range(0, d_conv, LANES):
            acc = jnp.broadcast_to(cb_ref[:, c0:c0 + LANES], (rt, LANES))
            for j in range(CONV_W):
                acc = acc + cw_ref[j:j + 1, c0:c0 + LANES] * up_scr[pl.ds(r0 + pad + j, rt),
                                                                    c0:c0 + LANES]
            cv_scr[r0:r0 + rt, c0:c0 + LANES] = acc
    cv = cv_scr[...]
    mu = jnp.mean(cv, axis=-1, keepdims=True)
    dv = cv - mu
    var = jnp.mean(dv * dv, axis=-1, keepdims=True)
    yc = _silu(dv * lax.rsqrt(var + EPS) * lg_ref[...] + lb_ref[...])
    yc = (yc * sgc_ref[0].astype(F32)).astype(BF16)
    y_c = jnp.dot(yc, wc_ref[...], preferred_element_type=F32)
    ya = (o_ref[0].astype(F32) * sga_ref[0].astype(F32)).astype(BF16)
    y_a = jnp.dot(ya, wa_ref[...], preferred_element_type=F32)
    gm = gm_ref[0].astype(F32)
    mix = (gm[:, :d_model] * y_c + gm[:, d_model:] * y_a).astype(BF16)
    y = jnp.dot(mix, wo_ref[...], preferred_element_type=F32)
    y = y * lax.rsqrt(jnp.mean(y * y, axis=-1, keepdims=True) + EPS) * gp_ref[...]
    y_ref[0] = x_ref[0] + mod_ref[0, 2:3, :] * y


def _post(x, mod3, u, prev, sgc, o, sga, gm, conv_w, conv_b, ln_g, ln_b, wc, wa, wo, g_post):
    b, l, d = x.shape
    d_conv = u.shape[2]
    d_attn = o.shape[2]
    tt = min(256, l)
    hb = tt // CONV_HALO

    def row(width):
        return pl.BlockSpec((1, tt, width), lambda bi, i: (bi, i, 0))

    def const(shape):
        return pl.BlockSpec(shape, lambda bi, i: (0,) * len(shape))

    return pl.pallas_call(
        functools.partial(_post_kernel, tt=tt, d_conv=d_conv, d_model=d),
        grid=(b, l // tt),
        in_specs=[
            row(d),
            pl.BlockSpec((1, 3, d), lambda bi, i: (bi, 0, 0)),
            row(d_conv),
            pl.BlockSpec((1, CONV_HALO, d_conv), lambda bi, i: (bi, jnp.maximum(i * hb - 1, 0), 0)),
            pl.BlockSpec((1, CONV_HALO, d_conv), lambda bi, i: (bi, 0, 0)),
            row(d_conv), row(d_attn), row(d_attn), row(2 * d),
            const((CONV_W, d_conv)), const((1, d_conv)), const((1, d_conv)), const((1, d_conv)),
            const((d_conv, d)), const((d_attn, d)), const((d, d)), const((1, d)),
        ],
        out_specs=row(d),
        out_shape=jax.ShapeDtypeStruct((b, l, d), F32),
        scratch_shapes=[pltpu.VMEM((tt + CONV_HALO, d_conv), F32), pltpu.VMEM((tt, d_conv), F32)],
        compiler_params=_params("arbitrary", "arbitrary"),
        name="post",
    )(x, mod3, u, u, prev, sgc, o, sga, gm, conv_w, conv_b, ln_g, ln_b, wc, wa, wo, g_post)


def _pad_axis(t, axis, size):
    if t.shape[axis] == size:
        return t
    widths = [(0, 0)] * t.ndim
    widths[axis] = (0, size - t.shape[axis])
    return jnp.pad(t, widths)


def _sublayer(x, mod3, conv_prev, past, tb, tri, w):
    b, l, d = x.shape
    d_conv = w["conv_w"].shape[1]
    d_attn = N_HEADS * HEAD_DIM
    (u, sgc, q, k, v, k_hm, vb, sga, qi, ki, kib, wi, gm) = _inproj(
        x, mod3, w["g_pre"], w["w_in"], d_conv=d_conv, d_attn=d_attn)

    if past is None:
        p_len = 0
    else:
        ck, cv, cki = past
        p_len = ck.shape[1]
        k_hm = jnp.concatenate([ck.transpose(0, 2, 1, 3).astype(BF16), k_hm], axis=2)
        vb = jnp.concatenate([cv.reshape(b, p_len, -1).astype(BF16), vb], axis=1)
        kib = jnp.concatenate([cki.astype(BF16), kib], axis=1)
    lk = p_len + l
    ksel = min(TOPK_MAX, lk // 4)
    tq = 256 if l % 256 == 0 else 128
    nq = -(-l // tq)
    lq_pad = nq * tq
    lk_pad = -(-(p_len + lq_pad) // KEY_BLOCK) * KEY_BLOCK
    nkb = lk_pad // KEY_BLOCK
    assert p_len % KEY_BLOCK == 0 and (tq == KEY_BLOCK or nq == 1)
    assert lk % CHUNK == 0 and ksel <= KEY_BLOCK
    qt = _pad_axis(q, 1, lq_pad).reshape(b, nq, tq, N_KV_HEADS, GROUP, HEAD_DIM)
    qt = qt.transpose(0, 3, 1, 5, 4, 2).reshape(b, N_KV_HEADS, nq, HEAD_DIM, GROUP * tq)
    qit = _pad_axis(qi, 1, lq_pad).reshape(b, nq, tq, IDX_HEADS, IDX_DIM)
    qit = qit.transpose(0, 1, 4, 3, 2).reshape(b, nq, IDX_DIM, IDX_HEADS * tq)
    vt = _pad_axis(vb, 1, lk_pad).reshape(b, nkb, KEY_BLOCK, N_KV_HEADS, HEAD_DIM)
    vt = vt.transpose(0, 3, 1, 4, 2)
    ones_row = jnp.zeros((b, N_KV_HEADS, nkb, PACKED_ROWS, KEY_BLOCK), BF16).at[:, :, :, 0].set(1.0)
    vt = jnp.concatenate([vt, ones_row], axis=3)
    o_t = _attention(qt, qit, _pad_axis(wi.transpose(0, 2, 1), 2, lq_pad),
                     _pad_axis(k_hm, 2, lk_pad), vt, _pad_axis(kib, 1, lk_pad),
                     tb, tri, tq=tq, past=p_len, ksel=ksel)
    o = o_t[:, :, :, :l].transpose(0, 3, 1, 2).reshape(b, l, d_attn)

    prev = jnp.pad(conv_prev, ((0, 0), (CONV_HALO - (CONV_W - 1), 0), (0, 0)))
    y = _post(x, mod3, u, prev, sgc, o, sga, gm, w["conv_w"], w["conv_b"], w["ln_g"], w["ln_b"],
              w["w_conv_out"], w["w_attn_out"], w["w_out"], w["g_post"])
    conv_new = jnp.concatenate([conv_prev, u], axis=1)[:, -(CONV_W - 1):]
    return (y, conv_new, k.reshape(b, l, N_KV_HEADS, HEAD_DIM), v.reshape(b, l, N_KV_HEADS, HEAD_DIM),
            ki)


def _pack_w_in(w_in, d_conv, d_attn, d_model):
    kvd = N_KV_HEADS * HEAD_DIM
    splits = (2 * d_conv, d_conv, d_attn, kvd, kvd, d_attn, IDX_HEADS * IDX_DIM, IDX_DIM,
              IDX_HEADS, 2 * d_model)
    offs = [0]
    for s in splits:
        offs.append(offs[-1] + s)
    seg = [w_in[:, offs[j]:offs[j + 1]] for j in range(len(splits))]
    kw = jnp.concatenate([seg[7], seg[8]], axis=1)
    kw = jnp.pad(kw, ((0, 0), (0, LANES - kw.shape[1])))
    return jnp.concatenate(seg[:7] + [kw, seg[9]], axis=1).astype(BF16)


def kernel(x_prompt, x_sample, c_prompt, c_sample, cache_k, cache_v, cache_kidx, state_conv,
           rel_bias, ada_w, ada_b, norm_pre, norm_post, w_in, conv_w, conv_b, conv_ln_g,
           conv_ln_b, w_conv_out, w_attn_out, w_out):
    depth = ada_w.shape[0]
    bp, _, d = x_prompt.shape
    d_conv = conv_w.shape[2]
    d_attn = N_HEADS * HEAD_DIM
    mod = _ada(jnp.concatenate([c_prompt, c_sample], axis=0), ada_w, ada_b)
    mod = mod.reshape(depth, mod.shape[1], 3, d)
    tb = _bias_tiles(rel_bias)
    tri = (jnp.arange(KEY_BLOCK)[:, None] >= jnp.arange(KEY_BLOCK)[None, :]).astype(BF16)

    xp, xs = x_prompt, x_sample
    outs = [[] for _ in range(8)]
    for l in range(depth):
        w = dict(
            g_pre=norm_pre[l][None], g_post=norm_post[l][None],
            w_in=_pack_w_in(w_in[l], d_conv, d_attn, d),
            conv_w=conv_w[l], conv_b=conv_b[l][None], ln_g=conv_ln_g[l][None],
            ln_b=conv_ln_b[l][None], w_conv_out=w_conv_out[l].astype(BF16),
            w_attn_out=w_attn_out[l].astype(BF16), w_out=w_out[l].astype(BF16))
        zeros_conv = jnp.zeros((bp, CONV_W - 1, d_conv), xp.dtype)
        xp, cp, kp, vp, kip = _sublayer(xp, mod[l, :bp], zeros_conv, None, tb, tri, w)
        xs, cs, kss, vss, kis = _sublayer(xs, mod[l, bp:], state_conv[l],
                                          (cache_k[l], cache_v[l], cache_kidx[l]), tb, tri, w)
        for lst, val in zip(outs, (kp, vp, kip, cp, kss, vss, kis, cs)):
            lst.append(val)
    return (xp, xs) + tuple(jnp.stack(o) for o in outs)
```

```python
import functools
import math

import jax
import jax.numpy as jnp
from jax import lax
from jax.experimental import pallas as pl
from jax.experimental.pallas import tpu as pltpu

F32 = jnp.float32
BF16 = jnp.bfloat16
I32 = jnp.int32
I16 = jnp.int16

CHUNK = 64
CONV_W = 31
N_HEADS = 16
N_KV_HEADS = 4
HEAD_DIM = 64
GROUP = N_HEADS // N_KV_HEADS
IDX_HEADS = 8
IDX_DIM = 64
TOPK_MAX = 256
N_BUCKETS = 32
MAX_DIST = 128
EPS = 1e-6

LANES = 128
SUBLANES = 8
PACKED_ROWS = 16
KEY_BLOCK = 256
BIAS_TILE = 128
CONV_HALO = 32
VMEM_LIMIT_BYTES = 56 * 1024 * 1024
NEG_INF = float("-inf")
INT_MIN = -2 ** 31
HALF_MIN = -2 ** 15
LOG2E = 1.4426950408889634


def _sigmoid(x):
    return 1.0 / (1.0 + jnp.exp(-x))


def _silu(x):
    return x * _sigmoid(x)


def _params(*sem):
    return pltpu.CompilerParams(dimension_semantics=sem, vmem_limit_bytes=VMEM_LIMIT_BYTES)


def _ada_kernel(c_ref, w_ref, b_ref, o_ref):
    c = c_ref[...]
    o_ref[0] = jnp.dot(_silu(c), w_ref[0], preferred_element_type=F32) + b_ref[0]


def _ada(c_all, ada_w, ada_b):
    depth, d, _ = ada_w.shape
    bc = c_all.shape[0]
    return pl.pallas_call(
        _ada_kernel,
        grid=(depth, 3),
        in_specs=[
            pl.BlockSpec((bc, d), lambda l, n: (0, 0)),
            pl.BlockSpec((1, d, d), lambda l, n: (l, 0, n)),
            pl.BlockSpec((1, 1, d), lambda l, n: (l, 0, n)),
        ],
        out_specs=pl.BlockSpec((1, bc, d), lambda l, n: (l, 0, n)),
        out_shape=jax.ShapeDtypeStruct((depth, bc, 3 * d), F32),
        compiler_params=_params("arbitrary", "arbitrary"),
        name="ada",
    )(c_all, ada_w, ada_b.reshape(depth, 1, 3 * d))


def _bias_kernel(rb_ref, o_ref):
    r = lax.broadcasted_iota(I32, (BIAS_TILE, BIAS_TILE), 0)
    c = lax.broadcasted_iota(I32, (BIAS_TILE, BIAS_TILE), 1)
    nb = N_BUCKETS // 2
    max_exact = nb // 2
    for t in range(2):
        rel = (t - 1) * BIAS_TILE + r - c
        ret = jnp.where(rel > 0, nb, 0)
        n = jnp.abs(rel)
        nf = jnp.maximum(n, 1).astype(F32)
        large = max_exact + (jnp.log(nf / max_exact) / math.log(MAX_DIST / max_exact)
                             * (nb - max_exact)).astype(I32)
        large = jnp.minimum(large, nb - 1)
        bucket = ret + jnp.where(n < max_exact, n, large)
        for h in range(N_HEADS):
            acc = jnp.zeros((BIAS_TILE, BIAS_TILE), F32)
            for b in range(N_BUCKETS):
                acc = jnp.where(bucket == b, rb_ref[b, h], acc)
            o_ref[h, t] = acc * LOG2E


def _bias_tiles(rel_bias):
    return pl.pallas_call(
        _bias_kernel,
        in_specs=[pl.BlockSpec(memory_space=pltpu.SMEM)],
        out_specs=pl.BlockSpec(memory_space=pltpu.VMEM),
        out_shape=jax.ShapeDtypeStruct((N_HEADS, 2, BIAS_TILE, BIAS_TILE), F32),
        name="bias_tiles",
    )(rel_bias)


def _seg_offsets(d_conv, d_attn):
    widths = (2 * d_conv, d_conv, d_attn, N_KV_HEADS * HEAD_DIM, N_KV_HEADS * HEAD_DIM, d_attn,
              IDX_HEADS * IDX_DIM, LANES, None)
    offs, acc = [], 0
    for w in widths:
        offs.append(acc)
        if w is not None:
            acc += w
    return offs


def _inproj_kernel(x_ref, mod_ref, g_ref, w_ref,
                   u_ref, sgc_ref, qt_ref, k_ref, v_ref, kb_ref, vt_ref, sga_ref, qit_ref,
                   ki_ref, kib_ref, wi_ref, gm_ref, *, tm, d_conv, d_attn, d_model):
    x = x_ref[0]
    ms = jnp.mean(x * x, axis=-1, keepdims=True)
    y = x * lax.rsqrt(ms + EPS) * g_ref[...]
    shift = mod_ref[0, 0:1, :]
    scale = mod_ref[0, 1:2, :]
    h = (y * (1.0 + scale) + shift).astype(BF16)
    o_a, o_gc, o_q, o_k, o_v, o_ga, o_qi, o_kw, o_gm = _seg_offsets(d_conv, d_attn)

    def mm(lo, width):
        return jnp.dot(h, w_ref[:, lo:lo + width], preferred_element_type=F32)

    a = mm(o_a, 2 * d_conv)
    u_ref[0] = a[:, :d_conv] * _sigmoid(a[:, d_conv:])
    sgc_ref[0] = _silu(mm(o_gc, d_conv)).astype(BF16)
    zqt = (mm(o_q, d_attn) * (HEAD_DIM ** -0.5 * LOG2E)).T
    for hh in range(N_HEADS):
        g, j = divmod(hh, GROUP)
        qt_ref[0, g, 0, :, j * tm:(j + 1) * tm] = (
            zqt[hh * HEAD_DIM:(hh + 1) * HEAD_DIM].astype(BF16))
    zk = mm(o_k, N_KV_HEADS * HEAD_DIM)
    zv = mm(o_v, N_KV_HEADS * HEAD_DIM)
    k_ref[0] = zk
    v_ref[0] = zv
    zvt = zv.T
    ones_rows = jnp.where(lax.broadcasted_iota(I32, (PACKED_ROWS, tm), 0) == 0, 1.0, 0.0)
    for g in range(N_KV_HEADS):
        kb_ref[0, g] = zk[:, g * HEAD_DIM:(g + 1) * HEAD_DIM].astype(BF16)
        vt_ref[0, g, 0:HEAD_DIM] = zvt[g * HEAD_DIM:(g + 1) * HEAD_DIM].astype(BF16)
        vt_ref[0, g, HEAD_DIM:] = ones_rows.astype(BF16)
    sga_ref[0] = _silu(mm(o_ga, d_attn)).astype(BF16)
    zqit = (mm(o_qi, IDX_HEADS * IDX_DIM) * (IDX_DIM ** -0.5)).T
    for hh in range(IDX_HEADS):
        qit_ref[0, 0, :, hh * tm:(hh + 1) * tm] = zqit[hh * IDX_DIM:(hh + 1) * IDX_DIM].astype(BF16)
    zkw = mm(o_kw, LANES)
    zki = zkw[:, :IDX_DIM]
    ki_ref[0] = zki
    kib_ref[0] = zki.astype(BF16)
    wi_ref[0] = zkw.T[IDX_DIM:IDX_DIM + IDX_HEADS] * (IDX_HEADS ** -0.5)
    gm_ref[0] = _sigmoid(mm(o_gm, 2 * d_model)).astype(BF16)


def _inproj(x, mod3, g_pre, w_pad, *, d_conv, d_attn):
    b, l, d = x.shape
    tm = min(256, l)
    nw = w_pad.shape[1]
    kvd = N_KV_HEADS * HEAD_DIM

    def row(width):
        return pl.BlockSpec((1, tm, width), lambda bi, i: (bi, i, 0))

    def heads(n, width):
        return pl.BlockSpec((1, n, tm, width), lambda bi, i: (bi, 0, i, 0))

    nq = l // tm
    vrows = HEAD_DIM + PACKED_ROWS
    out_shape = (
        jax.ShapeDtypeStruct((b, l, d_conv), F32),
        jax.ShapeDtypeStruct((b, l, d_conv), BF16),
        jax.ShapeDtypeStruct((b, N_KV_HEADS, nq, HEAD_DIM, GROUP * tm), BF16),
        jax.ShapeDtypeStruct((b, l, kvd), F32),
        jax.ShapeDtypeStruct((b, l, kvd), F32),
        jax.ShapeDtypeStruct((b, N_KV_HEADS, l, HEAD_DIM), BF16),
        jax.ShapeDtypeStruct((b, N_KV_HEADS, vrows, l), BF16),
        jax.ShapeDtypeStruct((b, l, d_attn), BF16),
        jax.ShapeDtypeStruct((b, nq, IDX_DIM, IDX_HEADS * tm), BF16),
        jax.ShapeDtypeStruct((b, l, IDX_DIM), F32),
        jax.ShapeDtypeStruct((b, l, IDX_DIM), BF16),
        jax.ShapeDtypeStruct((b, IDX_HEADS, l), F32),
        jax.ShapeDtypeStruct((b, l, 2 * d), BF16),
    )
    out_specs = (row(d_conv), row(d_conv),
                 pl.BlockSpec((1, N_KV_HEADS, 1, HEAD_DIM, GROUP * tm), lambda bi, i: (bi, 0, i, 0, 0)),
                 row(kvd), row(kvd), heads(N_KV_HEADS, HEAD_DIM),
                 pl.BlockSpec((1, N_KV_HEADS, vrows, tm), lambda bi, i: (bi, 0, 0, i)),
                 row(d_attn),
                 pl.BlockSpec((1, 1, IDX_DIM, IDX_HEADS * tm), lambda bi, i: (bi, i, 0, 0)),
                 row(IDX_DIM), row(IDX_DIM),
                 pl.BlockSpec((1, IDX_HEADS, tm), lambda bi, i: (bi, 0, i)),
                 row(2 * d))
    return pl.pallas_call(
        functools.partial(_inproj_kernel, tm=tm, d_conv=d_conv, d_attn=d_attn, d_model=d),
        grid=(b, l // tm),
        in_specs=[
            pl.BlockSpec((1, tm, d), lambda bi, i: (bi, i, 0)),
            pl.BlockSpec((1, 3, d), lambda bi, i: (bi, 0, 0)),
            pl.BlockSpec((1, d), lambda bi, i: (0, 0)),
            pl.BlockSpec((d, nw), lambda bi, i: (0, 0), pipeline_mode=pl.Buffered(1)),
        ],
        out_specs=out_specs,
        out_shape=out_shape,
        compiler_params=_params("arbitrary", "arbitrary"),
        name="inproj",
    )(x, mod3, g_pre, w_pad)


def _attn_kernel(qt_ref, qit_ref, wi_ref, k_ref, vt_ref, ki_ref, tb_ref, tri_ref, o_ref,
                 key_scr, hi_scr, lo_scr, mb_scr, m_scr, acc_scr, *, tq, past, ksel):
    i = pl.program_id(1)
    q0 = past + i * tq
    nkb = (q0 + tq + KEY_BLOCK - 1) // KEY_BLOCK

    def key_rows(kb):
        return pl.ds(pl.multiple_of(kb * KEY_BLOCK, KEY_BLOCK), KEY_BLOCK)

    qit = qit_ref[0, 0]
    wi = wi_ref[0]
    last = nkb - 1
    last_ok = (lax.broadcasted_iota(I32, (KEY_BLOCK, tq), 0) // CHUNK
               <= lax.broadcasted_iota(I32, (KEY_BLOCK, tq), 1) // CHUNK)

    def score_block(kb, is_last):
        s = jnp.dot(ki_ref[0, key_rows(kb), :], qit, preferred_element_type=F32)
        acc = jnp.zeros((KEY_BLOCK, tq), F32)
        for h in range(IDX_HEADS):
            acc = acc + jnp.maximum(s[:, h * tq:(h + 1) * tq], 0.0) * wi[h:h + 1, :]
        sc = jnp.where(last_ok, acc, NEG_INF) if is_last else acc
        bits = pltpu.bitcast(sc, I32)
        key = bits ^ ((bits >> 31) & 0x7FFFFFFF)
        key_scr[kb] = key
        hi_scr[kb] = (key >> 16).astype(I16)
        lo_scr[kb] = ((key & 0xFFFF) + HALF_MIN).astype(I16)

    def over_blocks(fn, init=0):
        carry = lax.fori_loop(0, last, lambda kb, c: fn(kb, False, c), init)
        return fn(last, True, carry)

    over_blocks(lambda kb, is_last, c: score_block(kb, is_last) or c)

    def count16(scr, pred):
        def body(kb, acc):
            hit = jnp.where(pred(scr[kb]), jnp.int16(1), jnp.int16(0))
            hit = hit.reshape(KEY_BLOCK // PACKED_ROWS, PACKED_ROWS, tq)
            parts = [hit[r] for r in range(KEY_BLOCK // PACKED_ROWS)]
            while len(parts) > 1:
                parts = [parts[r] + parts[r + 1] for r in range(0, len(parts), 2)]
            return acc + parts[0]
        acc = lax.fori_loop(0, nkb, body, jnp.zeros((PACKED_ROWS, tq), I16))
        return acc.astype(I32).sum(axis=0, keepdims=True)

    def search16(scr, target):
        def refine(it, thr):
            cand = thr + jnp.left_shift(jnp.int32(1), 15 - it)
            c16 = cand.astype(I16)
            return jnp.where(count16(scr, lambda x: x >= c16) >= target, cand, thr)
        return lax.fori_loop(0, 16, refine, jnp.full((1, tq), HALF_MIN, I32))

    thr_hi = search16(hi_scr, ksel)
    thr_hi16 = thr_hi.astype(I16)
    above = count16(hi_scr, lambda x: x > thr_hi16)

    def restrict(kb, carry):
        lo_scr[kb] = jnp.where(hi_scr[kb] == thr_hi16, lo_scr[kb], jnp.int16(HALF_MIN))
        return carry

    lax.fori_loop(0, nkb, restrict, 0)
    thr_lo = search16(lo_scr, ksel - above)
    thr = thr_hi * 65536 + (thr_lo - HALF_MIN)

    thr_lo16 = thr_lo.astype(I16)
    n_ge = above + count16(lo_scr, lambda x: x >= thr_lo16)
    surplus = jnp.max(n_ge) > ksel

    @pl.when(jnp.logical_not(surplus))
    def _():
        def mask_block(kb, is_last, carry):
            keep = jnp.where(key_scr[kb] >= thr, 0.0, NEG_INF)
            mb_scr[kb] = (jnp.where(last_ok, keep, NEG_INF) if is_last else keep).astype(BF16)
            return carry

        over_blocks(mask_block)

    @pl.when(surplus)
    def _():
        def count_gt(kb, acc):
            hit = (key_scr[kb] > thr).astype(I32)
            return acc + hit.reshape(KEY_BLOCK // 8, 8, tq).sum(axis=0)

        n_gt = lax.fori_loop(0, nkb, count_gt, jnp.zeros((8, tq), I32)).sum(axis=0, keepdims=True)
        need = (ksel - n_gt).astype(F32)

        def mask_block(kb, is_last, ties_before):
            kk = key_scr[kb]
            eq = kk == thr
            rank = jnp.dot(tri_ref[...], jnp.where(eq, 1.0, 0.0).astype(BF16),
                           preferred_element_type=F32) + ties_before
            tie = jnp.where(eq, jnp.where(rank <= need, 0.0, NEG_INF), NEG_INF)
            keep = jnp.where(kk > thr, 0.0, tie)
            mb_scr[kb] = (jnp.where(last_ok, keep, NEG_INF) if is_last else keep).astype(BF16)
            return rank[KEY_BLOCK - 1:KEY_BLOCK, :]

        over_blocks(mask_block, jnp.zeros((1, tq), F32))

    cols = GROUP * tq
    nsub = tq // BIAS_TILE

    def bias_block(h, diag):
        same = tb_ref[h, 1]
        prev = tb_ref[h, 0]
        far = jnp.broadcast_to(tb_ref[h, 0, 0:1, 0:1], (BIAS_TILE, BIAS_TILE))
        if diag:
            grid = [[same, prev], [far, same]]
        else:
            grid = [[far, far], [prev, far]]
        return jnp.concatenate([jnp.concatenate(r[:nsub], axis=1) for r in grid], axis=0)

    def far_row(g):
        return jnp.concatenate(
            [jnp.broadcast_to(tb_ref[g * GROUP + j, 0, 0:1, 0:1], (1, tq)) for j in range(GROUP)],
            axis=1)

    def step(kb, kind, carry):
        mb = mb_scr[kb]
        logits = []
        for g in range(N_KV_HEADS):
            s = jnp.dot(k_ref[0, g, key_rows(kb), :], qt_ref[0, g, 0],
                        preferred_element_type=F32).astype(BF16)
            if kind == "far":
                shift = far_row(g)
                t = jnp.concatenate([s[:, j * tq:(j + 1) * tq] + mb for j in range(GROUP)], axis=1)
            else:
                shift = jnp.zeros((1, cols), F32)
                gate = jnp.where(nkb >= 2, 0.0, NEG_INF) if kind == "off" else 0.0
                t = jnp.concatenate(
                    [s[:, j * tq:(j + 1) * tq] + mb
                     + (bias_block(g * GROUP + j, kind == "diag") + gate).astype(BF16)
                     for j in range(GROUP)], axis=1)
            logits.append((t, shift, jnp.max(t, axis=0, keepdims=True).astype(F32) + shift))
        for g in range(N_KV_HEADS):
            m = m_scr[g]
            t, shift, m_blk = logits[g]
            m_new = jnp.maximum(m, m_blk)
            sub = (jnp.where(m_new == NEG_INF, 0.0, m_new) - shift).astype(BF16)
            m_used = sub.astype(F32) + shift
            p = jnp.exp2(t - sub)
            acc_scr[g] = jnp.exp2(m - m_used) * acc_scr[g] + jnp.dot(
                vt_ref[0, g, kb], p, preferred_element_type=F32)
            m_scr[g] = jnp.where(m_new == NEG_INF, NEG_INF, m_used)
        return carry

    m_scr[...] = jnp.full(m_scr.shape, NEG_INF, F32)
    acc_scr[...] = jnp.zeros(acc_scr.shape, F32)
    lax.fori_loop(0, nkb - 2, lambda kb, c: step(kb, "far", c), 0)
    step(jnp.maximum(nkb - 2, 0), "off", 0)
    step(nkb - 1, "diag", 0)
    heads_t = []
    for g in range(N_KV_HEADS):
        acc = acc_scr[g]
        o = acc[:HEAD_DIM] / acc[HEAD_DIM:HEAD_DIM + 1]
        heads_t += [o[:, j * tq:(j + 1) * tq] for j in range(GROUP)]
    o_ref[0] = jnp.concatenate(heads_t, axis=0).T.astype(o_ref.dtype)


def _attention(qt, qit, wi_t, k_hm, vt, kib, tb, tri, *, tq, past, ksel):
    b, _, nq, _, _ = qt.shape
    lk = k_hm.shape[2]
    nkb_max = lk // KEY_BLOCK
    blocks = (nkb_max, KEY_BLOCK, tq)
    return pl.pallas_call(
        functools.partial(_attn_kernel, tq=tq, past=past, ksel=ksel),
        grid=(b, nq),
        in_specs=[
            pl.BlockSpec((1, N_KV_HEADS, 1, HEAD_DIM, GROUP * tq), lambda bi, i: (bi, 0, i, 0, 0)),
            pl.BlockSpec((1, 1, IDX_DIM, IDX_HEADS * tq), lambda bi, i: (bi, i, 0, 0)),
            pl.BlockSpec((1, IDX_HEADS, tq), lambda bi, i: (bi, 0, i)),
            pl.BlockSpec((1, N_KV_HEADS, lk, HEAD_DIM), lambda bi, i: (bi, 0, 0, 0)),
            pl.BlockSpec((1, N_KV_HEADS, nkb_max, vt.shape[3], KEY_BLOCK),
                         lambda bi, i: (bi, 0, 0, 0, 0)),
            pl.BlockSpec((1, lk, IDX_DIM), lambda bi, i: (bi, 0, 0)),
            pl.BlockSpec((N_HEADS, 2, BIAS_TILE, BIAS_TILE), lambda bi, i: (0, 0, 0, 0)),
            pl.BlockSpec((KEY_BLOCK, KEY_BLOCK), lambda bi, i: (0, 0)),
        ],
        out_specs=pl.BlockSpec((1, tq, N_HEADS * HEAD_DIM), lambda bi, i: (bi, i, 0)),
        out_shape=jax.ShapeDtypeStruct((b, nq * tq, N_HEADS * HEAD_DIM), BF16),
        scratch_shapes=[pltpu.VMEM(blocks, I32), pltpu.VMEM(blocks, I16), pltpu.VMEM(blocks, I16),
                        pltpu.VMEM(blocks, BF16),
                        pltpu.VMEM((N_KV_HEADS, 1, GROUP * tq), F32),
                        pltpu.VMEM((N_KV_HEADS, vt.shape[3], GROUP * tq), F32)],
        compiler_params=_params("arbitrary", "arbitrary"),
        name="attention",
    )(qt, qit, wi_t, k_hm, vt, kib, tb, tri)


def _post_kernel(x_ref, mod_ref, u_ref, halo_ref, prev_ref, sgc_ref, o_ref, sga_ref, gm_ref,
                 cw_ref, cb_ref, lg_ref, lb_ref, wc_ref, wa_ref, wo_ref, gp_ref,
                 y_ref, up_scr, sh_scr, cv_scr, *, tt, d_conv, d_model):
    i = pl.program_id(1)
    up_scr[0:CONV_HALO] = jnp.where(i == 0, prev_ref[0], halo_ref[0])
    up_scr[CONV_HALO:] = u_ref[0]
    span = tt + CONV_HALO - SUBLANES
    for s in range(1, SUBLANES):
        sh_scr[s - 1] = up_scr[pl.ds(s, span), :]
    pad = CONV_HALO - (CONV_W - 1)
    rt = min(tt, 128)
    for r0 in range(0, tt, rt):
        for c0 in range(0, d_conv, LANES):
            acc = jnp.broadcast_to(cb_ref[:, c0:c0 + LANES], (rt, LANES))
            for j in range(CONV_W):
                a, s = divmod(pad + j, SUBLANES)
                rows = pl.ds(r0 + SUBLANES * a, rt)
                win = up_scr[rows, c0:c0 + LANES] if s == 0 else sh_scr[s - 1, rows, c0:c0 + LANES]
                acc = acc + cw_ref[j:j + 1, c0:c0 + LANES] * win
            cv_scr[r0:r0 + rt, c0:c0 + LANES] = acc
    cv = cv_scr[...]
    mu = jnp.mean(cv, axis=-1, keepdims=True)
    dv = cv - mu
    var = jnp.mean(dv * dv, axis=-1, keepdims=True)
    yc = _silu(dv * lax.rsqrt(var + EPS) * lg_ref[...] + lb_ref[...])
    yc = (yc * sgc_ref[0].astype(F32)).astype(BF16)
    y_c = jnp.dot(yc, wc_ref[...], preferred_element_type=F32)
    ya = (o_ref[0].astype(F32) * sga_ref[0].astype(F32)).astype(BF16)
    y_a = jnp.dot(ya, wa_ref[...], preferred_element_type=F32)
    gm = gm_ref[0].astype(F32)
    mix = (gm[:, :d_model] * y_c + gm[:, d_model:] * y_a).astype(BF16)
    y = jnp.dot(mix, wo_ref[...], preferred_element_type=F32)
    y = y * lax.rsqrt(jnp.mean(y * y, axis=-1, keepdims=True) + EPS) * gp_ref[...]
    y_ref[0] = x_ref[0] + mod_ref[0, 2:3, :] * y


def _post(x, mod3, u, prev, sgc, o, sga, gm, conv_w, conv_b, ln_g, ln_b, wc, wa, wo, g_post):
    b, l, d = x.shape
    d_conv = u.shape[2]
    d_attn = o.shape[2]
    tt = min(256, l)
    hb = tt // CONV_HALO

    def row(width):
        return pl.BlockSpec((1, tt, width), lambda bi, i: (bi, i, 0))

    def const(shape):
        return pl.BlockSpec(shape, lambda bi, i: (0,) * len(shape))

    return pl.pallas_call(
        functools.partial(_post_kernel, tt=tt, d_conv=d_conv, d_model=d),
        grid=(b, l // tt),
        in_specs=[
            row(d),
            pl.BlockSpec((1, 3, d), lambda bi, i: (bi, 0, 0)),
            row(d_conv),
            pl.BlockSpec((1, CONV_HALO, d_conv), lambda bi, i: (bi, jnp.maximum(i * hb - 1, 0), 0)),
            pl.BlockSpec((1, CONV_HALO, d_conv), lambda bi, i: (bi, 0, 0)),
            row(d_conv), row(d_attn), row(d_attn), row(2 * d),
            const((CONV_W, d_conv)), const((1, d_conv)), const((1, d_conv)), const((1, d_conv)),
            const((d_conv, d)), const((d_attn, d)), const((d, d)), const((1, d)),
        ],
        out_specs=row(d),
        out_shape=jax.ShapeDtypeStruct((b, l, d), F32),
        scratch_shapes=[pltpu.VMEM((tt + CONV_HALO, d_conv), F32),
                        pltpu.VMEM((SUBLANES - 1, tt + CONV_HALO - SUBLANES, d_conv), F32),
                        pltpu.VMEM((tt, d_conv), F32)],
        compiler_params=_params("arbitrary", "arbitrary"),
        name="post",
    )(x, mod3, u, u, prev, sgc, o, sga, gm, conv_w, conv_b, ln_g, ln_b, wc, wa, wo, g_post)


def _pad_axis(t, axis, size):
    if t.shape[axis] == size:
        return t
    widths = [(0, 0)] * t.ndim
    widths[axis] = (0, size - t.shape[axis])
    return jnp.pad(t, widths)


def _sublayer(x, l_true, mod3, conv_prev, past, tb, tri, w):
    b, l, d = x.shape
    d_conv = w["conv_w"].shape[1]
    d_attn = N_HEADS * HEAD_DIM
    (u, sgc, qt, k, v, k_hm, vt, sga, qit, ki, kib, wi_t, gm) = _inproj(
        x, mod3, w["g_pre"], w["w_in"], d_conv=d_conv, d_attn=d_attn)

    if past is None:
        p_len = 0
    else:
        ck, cv, cki = past
        p_len = ck.shape[1]
        k_hm = jnp.concatenate([ck.transpose(0, 2, 1, 3).astype(BF16), k_hm], axis=2)
        ones_rows = jnp.zeros((b, N_KV_HEADS, PACKED_ROWS, p_len), BF16).at[:, :, 0].set(1.0)
        cvt = jnp.concatenate([cv.transpose(0, 2, 3, 1).astype(BF16), ones_rows], axis=2)
        vt = jnp.concatenate([cvt, vt], axis=3)
        kib = jnp.concatenate([cki.astype(BF16), kib], axis=1)
    ksel = min(TOPK_MAX, (p_len + l_true) // 4)
    tq = min(KEY_BLOCK, l)
    lk_pad = -(-(p_len + l) // KEY_BLOCK) * KEY_BLOCK
    nkb = lk_pad // KEY_BLOCK
    assert p_len % KEY_BLOCK == 0 and (tq == KEY_BLOCK or l == tq)
    assert (p_len + l_true) % CHUNK == 0 and ksel <= KEY_BLOCK
    vt = _pad_axis(vt, 3, lk_pad).reshape(b, N_KV_HEADS, vt.shape[2], nkb, KEY_BLOCK)
    o = _attention(qt, qit, wi_t, _pad_axis(k_hm, 2, lk_pad), vt.transpose(0, 1, 3, 2, 4),
                   _pad_axis(kib, 1, lk_pad), tb, tri, tq=tq, past=p_len, ksel=ksel)

    prev = jnp.pad(conv_prev, ((0, 0), (CONV_HALO - (CONV_W - 1), 0), (0, 0)))
    y = _post(x, mod3, u, prev, sgc, o, sga, gm, w["conv_w"], w["conv_b"], w["ln_g"], w["ln_b"],
              w["w_conv_out"], w["w_attn_out"], w["w_out"], w["g_post"])
    conv_new = jnp.concatenate([conv_prev, u[:, :l_true]], axis=1)[:, -(CONV_W - 1):]
    return (y, conv_new, k[:, :l_true].reshape(b, l_true, N_KV_HEADS, HEAD_DIM),
            v[:, :l_true].reshape(b, l_true, N_KV_HEADS, HEAD_DIM), ki[:, :l_true])


def _pack_w_in(w_in, d_conv, d_attn, d_model):
    kvd = N_KV_HEADS * HEAD_DIM
    splits = (2 * d_conv, d_conv, d_attn, kvd, kvd, d_attn, IDX_HEADS * IDX_DIM, IDX_DIM,
              IDX_HEADS, 2 * d_model)
    offs = [0]
    for s in splits:
        offs.append(offs[-1] + s)
    seg = [w_in[:, offs[j]:offs[j + 1]] for j in range(len(splits))]
    kw = jnp.concatenate([seg[7], seg[8]], axis=1)
    kw = jnp.pad(kw, ((0, 0), (0, LANES - kw.shape[1])))
    return jnp.concatenate(seg[:7] + [kw, seg[9]], axis=1).astype(BF16)


def kernel(x_prompt, x_sample, c_prompt, c_sample, cache_k, cache_v, cache_kidx, state_conv,
           rel_bias, ada_w, ada_b, norm_pre, norm_post, w_in, conv_w, conv_b, conv_ln_g,
           conv_ln_b, w_conv_out, w_attn_out, w_out):
    depth = ada_w.shape[0]
    bp, _, d = x_prompt.shape
    d_conv = conv_w.shape[2]
    d_attn = N_HEADS * HEAD_DIM
    mod = _ada(jnp.concatenate([c_prompt, c_sample], axis=0), ada_w, ada_b)
    mod = mod.reshape(depth, mod.shape[1], 3, d)
    tb = _bias_tiles(rel_bias)
    tri = (jnp.arange(KEY_BLOCK)[:, None] >= jnp.arange(KEY_BLOCK)[None, :]).astype(BF16)

    lp, ls = x_prompt.shape[1], x_sample.shape[1]
    xp, xs = x_prompt, _pad_axis(x_sample, 1, -(-ls // LANES) * LANES)
    outs = [[] for _ in range(8)]
    for l in range(depth):
        w = dict(
            g_pre=norm_pre[l][None], g_post=norm_post[l][None],
            w_in=_pack_w_in(w_in[l], d_conv, d_attn, d),
            conv_w=conv_w[l], conv_b=conv_b[l][None], ln_g=conv_ln_g[l][None],
            ln_b=conv_ln_b[l][None], w_conv_out=w_conv_out[l].astype(BF16),
            w_attn_out=w_attn_out[l].astype(BF16), w_out=w_out[l].astype(BF16))
        zeros_conv = jnp.zeros((bp, CONV_W - 1, d_conv), xp.dtype)
        xp, cp, kp, vp, kip = _sublayer(xp, lp, mod[l, :bp], zeros_conv, None, tb, tri, w)
        xs, cs, kss, vss, kis = _sublayer(xs, ls, mod[l, bp:], state_conv[l],
                                          (cache_k[l], cache_v[l], cache_kidx[l]), tb, tri, w)
        for lst, val in zip(outs, (kp, vp, kip, cp, kss, vss, kis, cs)):
            lst.append(val)
    return (xp, xs[:, :ls]) + tuple(jnp.stack(o) for o in outs)
```

```python
import functools
import math

import jax
import jax.numpy as jnp
from jax import lax
from jax.experimental import pallas as pl
from jax.experimental.pallas import tpu as pltpu

F32 = jnp.float32
BF16 = jnp.bfloat16
I32 = jnp.int32
I16 = jnp.int16

CHUNK = 64
CONV_W = 31
N_HEADS = 16
N_KV_HEADS = 4
HEAD_DIM = 64
GROUP = N_HEADS // N_KV_HEADS
IDX_HEADS = 8
IDX_DIM = 64
TOPK_MAX = 256
N_BUCKETS = 32
MAX_DIST = 128
EPS = 1e-6

LANES = 128
SUBLANES = 8
PACKED_ROWS = 16
KEY_BLOCK = 256
BIAS_TILE = 128
CONV_HALO = 32
VMEM_LIMIT_BYTES = 56 * 1024 * 1024
NEG_INF = float("-inf")
INT_MIN = -2 ** 31
HALF_MIN = -2 ** 15
LOG2E = 1.4426950408889634


def _sigmoid(x):
    return 1.0 / (1.0 + jnp.exp(-x))


def _silu(x):
    return x * _sigmoid(x)


def _params(*sem):
    return pltpu.CompilerParams(dimension_semantics=sem, vmem_limit_bytes=VMEM_LIMIT_BYTES)


def _ada_kernel(c_ref, w_ref, b_ref, o_ref):
    c = c_ref[...]
    o_ref[0] = jnp.dot(_silu(c), w_ref[0], preferred_element_type=F32) + b_ref[0]


def _ada(c_all, ada_w, ada_b):
    depth, d, _ = ada_w.shape
    bc = c_all.shape[0]
    return pl.pallas_call(
        _ada_kernel,
        grid=(depth, 3),
        in_specs=[
            pl.BlockSpec((bc, d), lambda l, n: (0, 0)),
            pl.BlockSpec((1, d, d), lambda l, n: (l, 0, n)),
            pl.BlockSpec((1, 1, d), lambda l, n: (l, 0, n)),
        ],
        out_specs=pl.BlockSpec((1, bc, d), lambda l, n: (l, 0, n)),
        out_shape=jax.ShapeDtypeStruct((depth, bc, 3 * d), F32),
        compiler_params=_params("arbitrary", "arbitrary"),
        name="ada",
    )(c_all, ada_w, ada_b.reshape(depth, 1, 3 * d))


def _bias_kernel(rb_ref, o_ref):
    r = lax.broadcasted_iota(I32, (BIAS_TILE, BIAS_TILE), 0)
    c = lax.broadcasted_iota(I32, (BIAS_TILE, BIAS_TILE), 1)
    nb = N_BUCKETS // 2
    max_exact = nb // 2
    for t in range(2):
        rel = (t - 1) * BIAS_TILE + r - c
        ret = jnp.where(rel > 0, nb, 0)
        n = jnp.abs(rel)
        nf = jnp.maximum(n, 1).astype(F32)
        large = max_exact + (jnp.log(nf / max_exact) / math.log(MAX_DIST / max_exact)
                             * (nb - max_exact)).astype(I32)
        large = jnp.minimum(large, nb - 1)
        bucket = ret + jnp.where(n < max_exact, n, large)
        for h in range(N_HEADS):
            acc = jnp.zeros((BIAS_TILE, BIAS_TILE), F32)
            for b in range(N_BUCKETS):
                acc = jnp.where(bucket == b, rb_ref[b, h], acc)
            o_ref[h, t] = acc * LOG2E


def _bias_tiles(rel_bias):
    return pl.pallas_call(
        _bias_kernel,
        in_specs=[pl.BlockSpec(memory_space=pltpu.SMEM)],
        out_specs=pl.BlockSpec(memory_space=pltpu.VMEM),
        out_shape=jax.ShapeDtypeStruct((N_HEADS, 2, BIAS_TILE, BIAS_TILE), F32),
        name="bias_tiles",
    )(rel_bias)


def _seg_offsets(d_conv, d_attn):
    widths = (2 * d_conv, d_conv, d_attn, N_KV_HEADS * HEAD_DIM, N_KV_HEADS * HEAD_DIM, d_attn,
              IDX_HEADS * IDX_DIM, LANES, None)
    offs, acc = [], 0
    for w in widths:
        offs.append(acc)
        if w is not None:
            acc += w
    return offs


def _inproj_kernel(x_ref, mod_ref, g_ref, w_ref,
                   u_ref, sgc_ref, qt_ref, k_ref, v_ref, kb_ref, vt_ref, sga_ref, qit_ref,
                   ki_ref, kib_ref, wi_ref, gm_ref, *, tm, d_conv, d_attn, d_model):
    x = x_ref[0]
    ms = jnp.mean(x * x, axis=-1, keepdims=True)
    y = x * lax.rsqrt(ms + EPS) * g_ref[...]
    shift = mod_ref[0, 0:1, :]
    scale = mod_ref[0, 1:2, :]
    h = (y * (1.0 + scale) + shift).astype(BF16)
    o_a, o_gc, o_q, o_k, o_v, o_ga, o_qi, o_kw, o_gm = _seg_offsets(d_conv, d_attn)

    def mm(lo, width):
        return jnp.dot(h, w_ref[:, lo:lo + width], preferred_element_type=F32)

    a = mm(o_a, 2 * d_conv)
    u_ref[0] = a[:, :d_conv] * _sigmoid(a[:, d_conv:])
    sgc_ref[0] = _silu(mm(o_gc, d_conv)).astype(BF16)
    zqt = (mm(o_q, d_attn) * (HEAD_DIM ** -0.5 * LOG2E)).T
    for hh in range(N_HEADS):
        g, j = divmod(hh, GROUP)
        qt_ref[0, g, 0, :, j * tm:(j + 1) * tm] = (
            zqt[hh * HEAD_DIM:(hh + 1) * HEAD_DIM].astype(BF16))
    zk = mm(o_k, N_KV_HEADS * HEAD_DIM)
    zv = mm(o_v, N_KV_HEADS * HEAD_DIM)
    k_ref[0] = zk
    v_ref[0] = zv
    zvt = zv.T
    ones_rows = jnp.where(lax.broadcasted_iota(I32, (PACKED_ROWS, tm), 0) == 0, 1.0, 0.0)
    for g in range(N_KV_HEADS):
        kb_ref[0, g] = zk[:, g * HEAD_DIM:(g + 1) * HEAD_DIM].astype(BF16)
        vt_ref[0, g, 0:HEAD_DIM] = zvt[g * HEAD_DIM:(g + 1) * HEAD_DIM].astype(BF16)
        vt_ref[0, g, HEAD_DIM:] = ones_rows.astype(BF16)
    sga_ref[0] = _silu(mm(o_ga, d_attn)).astype(BF16)
    zqit = (mm(o_qi, IDX_HEADS * IDX_DIM) * (IDX_DIM ** -0.5)).T
    for hh in range(IDX_HEADS):
        qit_ref[0, 0, :, hh * tm:(hh + 1) * tm] = zqit[hh * IDX_DIM:(hh + 1) * IDX_DIM].astype(BF16)
    zkw = mm(o_kw, LANES)
    zki = zkw[:, :IDX_DIM]
    ki_ref[0] = zki
    kib_ref[0] = zki.astype(BF16)
    wi_ref[0] = zkw.T[IDX_DIM:IDX_DIM + IDX_HEADS] * (IDX_HEADS ** -0.5)
    gm_ref[0] = _sigmoid(mm(o_gm, 2 * d_model)).astype(BF16)


def _inproj(x, mod3, g_pre, w_pad, *, d_conv, d_attn):
    b, l, d = x.shape
    tm = min(256, l)
    nw = w_pad.shape[1]
    kvd = N_KV_HEADS * HEAD_DIM

    def row(width):
        return pl.BlockSpec((1, tm, width), lambda bi, i: (bi, i, 0))

    def heads(n, width):
        return pl.BlockSpec((1, n, tm, width), lambda bi, i: (bi, 0, i, 0))

    nq = l // tm
    vrows = HEAD_DIM + PACKED_ROWS
    out_shape = (
        jax.ShapeDtypeStruct((b, l, d_conv), F32),
        jax.ShapeDtypeStruct((b, l, d_conv), BF16),
        jax.ShapeDtypeStruct((b, N_KV_HEADS, nq, HEAD_DIM, GROUP * tm), BF16),
        jax.ShapeDtypeStruct((b, l, kvd), F32),
        jax.ShapeDtypeStruct((b, l, kvd), F32),
        jax.ShapeDtypeStruct((b, N_KV_HEADS, l, HEAD_DIM), BF16),
        jax.ShapeDtypeStruct((b, N_KV_HEADS, vrows, l), BF16),
        jax.ShapeDtypeStruct((b, l, d_attn), BF16),
        jax.ShapeDtypeStruct((b, nq, IDX_DIM, IDX_HEADS * tm), BF16),
        jax.ShapeDtypeStruct((b, l, IDX_DIM), F32),
        jax.ShapeDtypeStruct((b, l, IDX_DIM), BF16),
        jax.ShapeDtypeStruct((b, IDX_HEADS, l), F32),
        jax.ShapeDtypeStruct((b, l, 2 * d), BF16),
    )
    out_specs = (row(d_conv), row(d_conv),
                 pl.BlockSpec((1, N_KV_HEADS, 1, HEAD_DIM, GROUP * tm), lambda bi, i: (bi, 0, i, 0, 0)),
                 row(kvd), row(kvd), heads(N_KV_HEADS, HEAD_DIM),
                 pl.BlockSpec((1, N_KV_HEADS, vrows, tm), lambda bi, i: (bi, 0, 0, i)),
                 row(d_attn),
                 pl.BlockSpec((1, 1, IDX_DIM, IDX_HEADS * tm), lambda bi, i: (bi, i, 0, 0)),
                 row(IDX_DIM), row(IDX_DIM),
                 pl.BlockSpec((1, IDX_HEADS, tm), lambda bi, i: (bi, 0, i)),
                 row(2 * d))
    return pl.pallas_call(
        functools.partial(_inproj_kernel, tm=tm, d_conv=d_conv, d_attn=d_attn, d_model=d),
        grid=(b, l // tm),
        in_specs=[
            pl.BlockSpec((1, tm, d), lambda bi, i: (bi, i, 0)),
            pl.BlockSpec((1, 3, d), lambda bi, i: (bi, 0, 0)),
            pl.BlockSpec((1, d), lambda bi, i: (0, 0)),
            pl.BlockSpec((d, nw), lambda bi, i: (0, 0), pipeline_mode=pl.Buffered(1)),
        ],
        out_specs=out_specs,
        out_shape=out_shape,
        compiler_params=_params("arbitrary", "arbitrary"),
        name="inproj",
    )(x, mod3, g_pre, w_pad)


def _attn_kernel(qt_ref, qit_ref, wi_ref, k_ref, vt_ref, ki_ref, tb_ref, tri_ref, o_ref,
                 key_scr, hi_scr, lo_scr, mb_scr, m_scr, acc_scr, ta_scr, sa_scr, tb_scr, sb_scr,
                 *, tq, past, ksel):
    i = pl.program_id(1)
    q0 = past + i * tq
    nkb = (q0 + tq + KEY_BLOCK - 1) // KEY_BLOCK

    def key_rows(kb):
        return pl.ds(pl.multiple_of(kb * KEY_BLOCK, KEY_BLOCK), KEY_BLOCK)

    qit = qit_ref[0, 0]
    wi = wi_ref[0]
    last = nkb - 1
    last_ok = (lax.broadcasted_iota(I32, (KEY_BLOCK, tq), 0) // CHUNK
               <= lax.broadcasted_iota(I32, (KEY_BLOCK, tq), 1) // CHUNK)

    def score_block(kb, is_last):
        s = jnp.dot(ki_ref[0, key_rows(kb), :], qit, preferred_element_type=F32)
        acc = jnp.zeros((KEY_BLOCK, tq), F32)
        for h in range(IDX_HEADS):
            acc = acc + jnp.maximum(s[:, h * tq:(h + 1) * tq], 0.0) * wi[h:h + 1, :]
        sc = jnp.where(last_ok, acc, NEG_INF) if is_last else acc
        bits = pltpu.bitcast(sc, I32)
        key = bits ^ ((bits >> 31) & 0x7FFFFFFF)
        key_scr[kb] = key
        hi_scr[kb] = (key >> 16).astype(I16)
        lo_scr[kb] = ((key & 0xFFFF) + HALF_MIN).astype(I16)

    def over_blocks(fn, init=0):
        carry = lax.fori_loop(0, last, lambda kb, c: fn(kb, False, c), init)
        return fn(last, True, carry)

    over_blocks(lambda kb, is_last, c: score_block(kb, is_last) or c)

    def count16(scr, pred):
        def body(kb, acc):
            hit = jnp.where(pred(scr[kb]), jnp.int16(1), jnp.int16(0))
            hit = hit.reshape(KEY_BLOCK // PACKED_ROWS, PACKED_ROWS, tq)
            parts = [hit[r] for r in range(KEY_BLOCK // PACKED_ROWS)]
            while len(parts) > 1:
                parts = [parts[r] + parts[r + 1] for r in range(0, len(parts), 2)]
            return acc + parts[0]
        acc = lax.fori_loop(0, nkb, body, jnp.zeros((PACKED_ROWS, tq), I16))
        return acc.astype(I32).sum(axis=0, keepdims=True)

    def search16(scr, target):
        def refine(it, thr):
            cand = thr + jnp.left_shift(jnp.int32(1), 15 - it)
            c16 = cand.astype(I16)
            return jnp.where(count16(scr, lambda x: x >= c16) >= target, cand, thr)
        return lax.fori_loop(0, 16, refine, jnp.full((1, tq), HALF_MIN, I32))

    thr_hi = search16(hi_scr, ksel)
    thr_hi16 = thr_hi.astype(I16)
    above = count16(hi_scr, lambda x: x > thr_hi16)

    def restrict(kb, carry):
        lo_scr[kb] = jnp.where(hi_scr[kb] == thr_hi16, lo_scr[kb], jnp.int16(HALF_MIN))
        return carry

    lax.fori_loop(0, nkb, restrict, 0)
    thr_lo = search16(lo_scr, ksel - above)
    thr = thr_hi * 65536 + (thr_lo - HALF_MIN)

    thr_lo16 = thr_lo.astype(I16)
    n_ge = above + count16(lo_scr, lambda x: x >= thr_lo16)
    surplus = jnp.max(n_ge) > ksel

    @pl.when(jnp.logical_not(surplus))
    def _():
        def mask_block(kb, is_last, carry):
            keep = jnp.where(key_scr[kb] >= thr, 0.0, NEG_INF)
            mb_scr[kb] = (jnp.where(last_ok, keep, NEG_INF) if is_last else keep).astype(BF16)
            return carry

        over_blocks(mask_block)

    @pl.when(surplus)
    def _():
        def count_gt(kb, acc):
            hit = (key_scr[kb] > thr).astype(I32)
            return acc + hit.reshape(KEY_BLOCK // 8, 8, tq).sum(axis=0)

        n_gt = lax.fori_loop(0, nkb, count_gt, jnp.zeros((8, tq), I32)).sum(axis=0, keepdims=True)
        need = (ksel - n_gt).astype(F32)

        def mask_block(kb, is_last, ties_before):
            kk = key_scr[kb]
            eq = kk == thr
            rank = jnp.dot(tri_ref[...], jnp.where(eq, 1.0, 0.0).astype(BF16),
                           preferred_element_type=F32) + ties_before
            tie = jnp.where(eq, jnp.where(rank <= need, 0.0, NEG_INF), NEG_INF)
            keep = jnp.where(kk > thr, 0.0, tie)
            mb_scr[kb] = (jnp.where(last_ok, keep, NEG_INF) if is_last else keep).astype(BF16)
            return rank[KEY_BLOCK - 1:KEY_BLOCK, :]

        over_blocks(mask_block, jnp.zeros((1, tq), F32))

    nsub = tq // BIAS_TILE

    def bias_block(h, diag):
        same = tb_ref[h, 1]
        prev = tb_ref[h, 0]
        far = jnp.broadcast_to(tb_ref[h, 0, 0:1, 0:1], (BIAS_TILE, BIAS_TILE))
        if diag:
            grid = [[same, prev], [far, same]]
        else:
            grid = [[far, far], [prev, far]]
        return jnp.concatenate([jnp.concatenate(r[:nsub], axis=1) for r in grid], axis=0)

    def produce(h, kb, kind, buf, gated=False):
        t_buf, s_buf = buf
        g, j = divmod(h, GROUP)
        lanes = slice(j * tq, (j + 1) * tq)
        t = jnp.dot(k_ref[0, g, key_rows(kb), :], qt_ref[0, g, 0, :, lanes],
                    preferred_element_type=F32).astype(BF16) + mb_scr[kb]
        if kind == "far":
            shift = jnp.broadcast_to(tb_ref[h, 0, 0:1, 0:1], (1, tq))
        else:
            shift = jnp.zeros((1, tq), F32)
            gate = jnp.where(nkb >= 2, 0.0, NEG_INF) if gated else 0.0
            t = t + (bias_block(h, kind == "diag") + gate).astype(BF16)
        t_buf[g, :, lanes] = t
        s_buf[g, 0:1, lanes] = jnp.max(t, axis=0, keepdims=True).astype(F32) + shift
        s_buf[g, 1:2, lanes] = shift

    def consume(h, kb, buf):
        t_buf, s_buf = buf
        g, j = divmod(h, GROUP)
        lanes = slice(j * tq, (j + 1) * tq)
        m = m_scr[g, :, lanes]
        shift = s_buf[g, 1:2, lanes]
        m_new = jnp.maximum(m, s_buf[g, 0:1, lanes])
        sub = (jnp.where(m_new == NEG_INF, 0.0, m_new) - shift).astype(BF16)
        m_used = sub.astype(F32) + shift
        p = jnp.exp2(t_buf[g, :, lanes] - sub)
        acc_scr[g, :, lanes] = jnp.exp2(m - m_used) * acc_scr[g, :, lanes] + jnp.dot(
            vt_ref[0, g, kb], p, preferred_element_type=F32)
        m_scr[g, :, lanes] = jnp.where(m_new == NEG_INF, NEG_INF, m_used)

    def stage(new=None, old=None):
        for h in range(N_HEADS):
            if new is not None:
                produce(h, *new)
            if old is not None:
                consume(h, *old)

    m_scr[...] = jnp.full(m_scr.shape, NEG_INF, F32)
    acc_scr[...] = jnp.zeros(acc_scr.shape, F32)

    buf_a, buf_b = (ta_scr, sa_scr), (tb_scr, sb_scr)
    n_far = jnp.maximum(nkb - 2, 0)
    off = jnp.maximum(nkb - 2, 0)
    stage(new=(last, "diag", buf_a))
    stage(new=(off, "off", buf_b, True), old=(last, buf_a))

    def far_pair(it, carry):
        f = nkb - 3 - 2 * it
        stage(new=(f, "far", buf_a), old=(f + 1, buf_b))
        stage(new=(f - 1, "far", buf_b), old=(f, buf_a))
        return carry

    lax.fori_loop(0, n_far // 2, far_pair, 0)

    @pl.when(n_far % 2 == 1)
    def _():
        stage(new=(0, "far", buf_a), old=(1, buf_b))
        stage(old=(0, buf_a))

    @pl.when(n_far % 2 == 0)
    def _():
        stage(old=(jnp.maximum(nkb - 2 - n_far, 0), buf_b))

    heads_t = []
    for g in range(N_KV_HEADS):
        acc = acc_scr[g]
        o = acc[:HEAD_DIM] / acc[HEAD_DIM:HEAD_DIM + 1]
        heads_t += [o[:, j * tq:(j + 1) * tq] for j in range(GROUP)]
    o_ref[0] = jnp.concatenate(heads_t, axis=0).T.astype(o_ref.dtype)


def _attention(qt, qit, wi_t, k_hm, vt, kib, tb, tri, *, tq, past, ksel):
    b, _, nq, _, _ = qt.shape
    lk = k_hm.shape[2]
    nkb_max = lk // KEY_BLOCK
    blocks = (nkb_max, KEY_BLOCK, tq)
    return pl.pallas_call(
        functools.partial(_attn_kernel, tq=tq, past=past, ksel=ksel),
        grid=(b, nq),
        in_specs=[
            pl.BlockSpec((1, N_KV_HEADS, 1, HEAD_DIM, GROUP * tq), lambda bi, i: (bi, 0, i, 0, 0)),
            pl.BlockSpec((1, 1, IDX_DIM, IDX_HEADS * tq), lambda bi, i: (bi, i, 0, 0)),
            pl.BlockSpec((1, IDX_HEADS, tq), lambda bi, i: (bi, 0, i)),
            pl.BlockSpec((1, N_KV_HEADS, lk, HEAD_DIM), lambda bi, i: (bi, 0, 0, 0)),
            pl.BlockSpec((1, N_KV_HEADS, nkb_max, vt.shape[3], KEY_BLOCK),
                         lambda bi, i: (bi, 0, 0, 0, 0)),
            pl.BlockSpec((1, lk, IDX_DIM), lambda bi, i: (bi, 0, 0)),
            pl.BlockSpec((N_HEADS, 2, BIAS_TILE, BIAS_TILE), lambda bi, i: (0, 0, 0, 0)),
            pl.BlockSpec((KEY_BLOCK, KEY_BLOCK), lambda bi, i: (0, 0)),
        ],
        out_specs=pl.BlockSpec((1, tq, N_HEADS * HEAD_DIM), lambda bi, i: (bi, i, 0)),
        out_shape=jax.ShapeDtypeStruct((b, nq * tq, N_HEADS * HEAD_DIM), BF16),
        scratch_shapes=[pltpu.VMEM(blocks, I32), pltpu.VMEM(blocks, I16), pltpu.VMEM(blocks, I16),
                        pltpu.VMEM(blocks, BF16),
                        pltpu.VMEM((N_KV_HEADS, 1, GROUP * tq), F32),
                        pltpu.VMEM((N_KV_HEADS, vt.shape[3], GROUP * tq), F32)]
        + 2 * [pltpu.VMEM((N_KV_HEADS, KEY_BLOCK, GROUP * tq), BF16),
               pltpu.VMEM((N_KV_HEADS, 2, GROUP * tq), F32)],
        compiler_params=_params("arbitrary", "arbitrary"),
        name="attention",
    )(qt, qit, wi_t, k_hm, vt, kib, tb, tri)


def _post_kernel(x_ref, mod_ref, u_ref, halo_ref, prev_ref, sgc_ref, o_ref, sga_ref, gm_ref,
                 cw_ref, cb_ref, lg_ref, lb_ref, wc_ref, wa_ref, wo_ref, gp_ref,
                 y_ref, up_scr, sh_scr, cv_scr, *, tt, d_conv, d_model):
    i = pl.program_id(1)
    up_scr[0:CONV_HALO] = jnp.where(i == 0, prev_ref[0], halo_ref[0])
    up_scr[CONV_HALO:] = u_ref[0]
    span = tt + CONV_HALO - SUBLANES
    for s in range(1, SUBLANES):
        sh_scr[s - 1] = up_scr[pl.ds(s, span), :]
    pad = CONV_HALO - (CONV_W - 1)
    rt = min(tt, 128)
    for r0 in range(0, tt, rt):
        for c0 in range(0, d_conv, LANES):
            acc = jnp.broadcast_to(cb_ref[:, c0:c0 + LANES], (rt, LANES))
            for j in range(CONV_W):
                a, s = divmod(pad + j, SUBLANES)
                rows = pl.ds(r0 + SUBLANES * a, rt)
                win = up_scr[rows, c0:c0 + LANES] if s == 0 else sh_scr[s - 1, rows, c0:c0 + LANES]
                acc = acc + cw_ref[j:j + 1, c0:c0 + LANES] * win
            cv_scr[r0:r0 + rt, c0:c0 + LANES] = acc
    cv = cv_scr[...]
    mu = jnp.mean(cv, axis=-1, keepdims=True)
    dv = cv - mu
    var = jnp.mean(dv * dv, axis=-1, keepdims=True)
    yc = _silu(dv * lax.rsqrt(var + EPS) * lg_ref[...] + lb_ref[...])
    yc = (yc * sgc_ref[0].astype(F32)).astype(BF16)
    y_c = jnp.dot(yc, wc_ref[...], preferred_element_type=F32)
    ya = (o_ref[0].astype(F32) * sga_ref[0].astype(F32)).astype(BF16)
    y_a = jnp.dot(ya, wa_ref[...], preferred_element_type=F32)
    gm = gm_ref[0].astype(F32)
    mix = (gm[:, :d_model] * y_c + gm[:, d_model:] * y_a).astype(BF16)
    y = jnp.dot(mix, wo_ref[...], preferred_element_type=F32)
    y = y * lax.rsqrt(jnp.mean(y * y, axis=-1, keepdims=True) + EPS) * gp_ref[...]
    y_ref[0] = x_ref[0] + mod_ref[0, 2:3, :] * y


def _post(x, mod3, u, prev, sgc, o, sga, gm, conv_w, conv_b, ln_g, ln_b, wc, wa, wo, g_post):
    b, l, d = x.shape
    d_conv = u.shape[2]
    d_attn = o.shape[2]
    tt = min(256, l)
    hb = tt // CONV_HALO

    def row(width):
        return pl.BlockSpec((1, tt, width), lambda bi, i: (bi, i, 0))

    def const(shape):
        return pl.BlockSpec(shape, lambda bi, i: (0,) * len(shape))

    return pl.pallas_call(
        functools.partial(_post_kernel, tt=tt, d_conv=d_conv, d_model=d),
        grid=(b, l // tt),
        in_specs=[
            row(d),
            pl.BlockSpec((1, 3, d), lambda bi, i: (bi, 0, 0)),
            row(d_conv),
            pl.BlockSpec((1, CONV_HALO, d_conv), lambda bi, i: (bi, jnp.maximum(i * hb - 1, 0), 0)),
            pl.BlockSpec((1, CONV_HALO, d_conv), lambda bi, i: (bi, 0, 0)),
            row(d_conv), row(d_attn), row(d_attn), row(2 * d),
            const((CONV_W, d_conv)), const((1, d_conv)), const((1, d_conv)), const((1, d_conv)),
            const((d_conv, d)), const((d_attn, d)), const((d, d)), const((1, d)),
        ],
        out_specs=row(d),
        out_shape=jax.ShapeDtypeStruct((b, l, d), F32),
        scratch_shapes=[pltpu.VMEM((tt + CONV_HALO, d_conv), F32),
                        pltpu.VMEM((SUBLANES - 1, tt + CONV_HALO - SUBLANES, d_conv), F32),
                        pltpu.VMEM((tt, d_conv), F32)],
        compiler_params=_params("arbitrary", "arbitrary"),
        name="post",
    )(x, mod3, u, u, prev, sgc, o, sga, gm, conv_w, conv_b, ln_g, ln_b, wc, wa, wo, g_post)


def _pad_axis(t, axis, size):
    if t.shape[axis] == size:
        return t
    widths = [(0, 0)] * t.ndim
    widths[axis] = (0, size - t.shape[axis])
    return jnp.pad(t, widths)


def _sublayer(x, l_true, mod3, conv_prev, past, tb, tri, w):
    b, l, d = x.shape
    d_conv = w["conv_w"].shape[1]
    d_attn = N_HEADS * HEAD_DIM
    (u, sgc, qt, k, v, k_hm, vt, sga, qit, ki, kib, wi_t, gm) = _inproj(
        x, mod3, w["g_pre"], w["w_in"], d_conv=d_conv, d_attn=d_attn)

    if past is None:
        p_len = 0
    else:
        ck, cv, cki = past
        p_len = ck.shape[1]
        k_hm = jnp.concatenate([ck.transpose(0, 2, 1, 3).astype(BF16), k_hm], axis=2)
        ones_rows = jnp.zeros((b, N_KV_HEADS, PACKED_ROWS, p_len), BF16).at[:, :, 0].set(1.0)
        cvt = jnp.concatenate([cv.transpose(0, 2, 3, 1).astype(BF16), ones_rows], axis=2)
        vt = jnp.concatenate([cvt, vt], axis=3)
        kib = jnp.concatenate([cki.astype(BF16), kib], axis=1)
    ksel = min(TOPK_MAX, (p_len + l_true) // 4)
    tq = min(KEY_BLOCK, l)
    lk_pad = -(-(p_len + l) // KEY_BLOCK) * KEY_BLOCK
    nkb = lk_pad // KEY_BLOCK
    assert p_len % KEY_BLOCK == 0 and (tq == KEY_BLOCK or l == tq)
    assert (p_len + l_true) % CHUNK == 0 and ksel <= KEY_BLOCK
    vt = _pad_axis(vt, 3, lk_pad).reshape(b, N_KV_HEADS, vt.shape[2], nkb, KEY_BLOCK)
    o = _attention(qt, qit, wi_t, _pad_axis(k_hm, 2, lk_pad), vt.transpose(0, 1, 3, 2, 4),
                   _pad_axis(kib, 1, lk_pad), tb, tri, tq=tq, past=p_len, ksel=ksel)

    prev = jnp.pad(conv_prev, ((0, 0), (CONV_HALO - (CONV_W - 1), 0), (0, 0)))
    y = _post(x, mod3, u, prev, sgc, o, sga, gm, w["conv_w"], w["conv_b"], w["ln_g"], w["ln_b"],
              w["w_conv_out"], w["w_attn_out"], w["w_out"], w["g_post"])
    conv_new = jnp.concatenate([conv_prev, u[:, :l_true]], axis=1)[:, -(CONV_W - 1):]
    return (y, conv_new, k[:, :l_true].reshape(b, l_true, N_KV_HEADS, HEAD_DIM),
            v[:, :l_true].reshape(b, l_true, N_KV_HEADS, HEAD_DIM), ki[:, :l_true])


def _pack_w_in(w_in, d_conv, d_attn, d_model):
    kvd = N_KV_HEADS * HEAD_DIM
    splits = (2 * d_conv, d_conv, d_attn, kvd, kvd, d_attn, IDX_HEADS * IDX_DIM, IDX_DIM,
              IDX_HEADS, 2 * d_model)
    offs = [0]
    for s in splits:
        offs.append(offs[-1] + s)
    seg = [w_in[:, offs[j]:offs[j + 1]] for j in range(len(splits))]
    kw = jnp.concatenate([seg[7], seg[8]], axis=1)
    kw = jnp.pad(kw, ((0, 0), (0, LANES - kw.shape[1])))
    return jnp.concatenate(seg[:7] + [kw, seg[9]], axis=1).astype(BF16)


def kernel(x_prompt, x_sample, c_prompt, c_sample, cache_k, cache_v, cache_kidx, state_conv,
           rel_bias, ada_w, ada_b, norm_pre, norm_post, w_in, conv_w, conv_b, conv_ln_g,
           conv_ln_b, w_conv_out, w_attn_out, w_out):
    depth = ada_w.shape[0]
    bp, _, d = x_prompt.shape
    d_conv = conv_w.shape[2]
    d_attn = N_HEADS * HEAD_DIM
    mod = _ada(jnp.concatenate([c_prompt, c_sample], axis=0), ada_w, ada_b)
    mod = mod.reshape(depth, mod.shape[1], 3, d)
    tb = _bias_tiles(rel_bias)
    tri = (jnp.arange(KEY_BLOCK)[:, None] >= jnp.arange(KEY_BLOCK)[None, :]).astype(BF16)

    lp, ls = x_prompt.shape[1], x_sample.shape[1]
    xp, xs = x_prompt, _pad_axis(x_sample, 1, -(-ls // LANES) * LANES)
    outs = [[] for _ in range(8)]
    for l in range(depth):
        w = dict(
            g_pre=norm_pre[l][None], g_post=norm_post[l][None],
            w_in=_pack_w_in(w_in[l], d_conv, d_attn, d),
            conv_w=conv_w[l], conv_b=conv_b[l][None], ln_g=conv_ln_g[l][None],
            ln_b=conv_ln_b[l][None], w_conv_out=w_conv_out[l].astype(BF16),
            w_attn_out=w_attn_out[l].astype(BF16), w_out=w_out[l].astype(BF16))
        zeros_conv = jnp.zeros((bp, CONV_W - 1, d_conv), xp.dtype)
        xp, cp, kp, vp, kip = _sublayer(xp, lp, mod[l, :bp], zeros_conv, None, tb, tri, w)
        xs, cs, kss, vss, kis = _sublayer(xs, ls, mod[l, bp:], state_conv[l],
                                          (cache_k[l], cache_v[l], cache_kidx[l]), tb, tri, w)
        for lst, val in zip(outs, (kp, vp, kip, cp, kss, vss, kis, cs)):
            lst.append(val)
    return (xp, xs[:, :ls]) + tuple(jnp.stack(o) for o in outs)
```

```python
import functools
import math

import jax
import jax.numpy as jnp
from jax import lax
from jax.experimental import pallas as pl
from jax.experimental.pallas import tpu as pltpu

F32 = jnp.float32
BF16 = jnp.bfloat16
I32 = jnp.int32
I16 = jnp.int16

CHUNK = 64
CONV_W = 31
N_HEADS = 16
N_KV_HEADS = 4
HEAD_DIM = 64
GROUP = N_HEADS // N_KV_HEADS
IDX_HEADS = 8
IDX_DIM = 64
TOPK_MAX = 256
N_BUCKETS = 32
MAX_DIST = 128
EPS = 1e-6

LANES = 128
SUBLANES = 8
PACKED_ROWS = 16
KEY_BLOCK = 256
BIAS_TILE = 128
CONV_HALO = 32
VMEM_LIMIT_BYTES = 56 * 1024 * 1024
NEG_INF = float("-inf")
INT_MIN = -2 ** 31
HALF_MIN = -2 ** 15
LOG2E = 1.4426950408889634


def _sigmoid(x):
    return 1.0 / (1.0 + jnp.exp(-x))


def _silu(x):
    return x * _sigmoid(x)


def _params(*sem):
    return pltpu.CompilerParams(dimension_semantics=sem, vmem_limit_bytes=VMEM_LIMIT_BYTES)


def _ada_kernel(c_ref, w_ref, b_ref, o_ref):
    c = c_ref[...]
    o_ref[0] = jnp.dot(_silu(c), w_ref[0], preferred_element_type=F32) + b_ref[0]


def _ada(c_all, ada_w, ada_b):
    depth, d, _ = ada_w.shape
    bc = c_all.shape[0]
    return pl.pallas_call(
        _ada_kernel,
        grid=(depth, 3),
        in_specs=[
            pl.BlockSpec((bc, d), lambda l, n: (0, 0)),
            pl.BlockSpec((1, d, d), lambda l, n: (l, 0, n)),
            pl.BlockSpec((1, 1, d), lambda l, n: (l, 0, n)),
        ],
        out_specs=pl.BlockSpec((1, bc, d), lambda l, n: (l, 0, n)),
        out_shape=jax.ShapeDtypeStruct((depth, bc, 3 * d), F32),
        compiler_params=_params("arbitrary", "arbitrary"),
        name="ada",
    )(c_all, ada_w, ada_b.reshape(depth, 1, 3 * d))


def _bias_kernel(rb_ref, o_ref):
    r = lax.broadcasted_iota(I32, (BIAS_TILE, BIAS_TILE), 0)
    c = lax.broadcasted_iota(I32, (BIAS_TILE, BIAS_TILE), 1)
    nb = N_BUCKETS // 2
    max_exact = nb // 2
    for t in range(2):
        rel = (t - 1) * BIAS_TILE + r - c
        ret = jnp.where(rel > 0, nb, 0)
        n = jnp.abs(rel)
        nf = jnp.maximum(n, 1).astype(F32)
        large = max_exact + (jnp.log(nf / max_exact) / math.log(MAX_DIST / max_exact)
                             * (nb - max_exact)).astype(I32)
        large = jnp.minimum(large, nb - 1)
        bucket = ret + jnp.where(n < max_exact, n, large)
        for h in range(N_HEADS):
            acc = jnp.zeros((BIAS_TILE, BIAS_TILE), F32)
            for b in range(N_BUCKETS):
                acc = jnp.where(bucket == b, rb_ref[b, h], acc)
            o_ref[h, t] = acc * LOG2E


def _bias_tiles(rel_bias):
    return pl.pallas_call(
        _bias_kernel,
        in_specs=[pl.BlockSpec(memory_space=pltpu.SMEM)],
        out_specs=pl.BlockSpec(memory_space=pltpu.VMEM),
        out_shape=jax.ShapeDtypeStruct((N_HEADS, 2, BIAS_TILE, BIAS_TILE), F32),
        name="bias_tiles",
    )(rel_bias)


def _seg_offsets(d_conv, d_attn):
    widths = (2 * d_conv, d_conv, d_attn, N_KV_HEADS * HEAD_DIM, N_KV_HEADS * HEAD_DIM, d_attn,
              IDX_HEADS * IDX_DIM, LANES, None)
    offs, acc = [], 0
    for w in widths:
        offs.append(acc)
        if w is not None:
            acc += w
    return offs


def _inproj_kernel(x_ref, mod_ref, g_ref, w_ref,
                   u_ref, sgc_ref, qt_ref, k_ref, v_ref, kb_ref, vt_ref, sga_ref, qit_ref,
                   ki_ref, kib_ref, wi_ref, gm_ref, *, tm, d_conv, d_attn, d_model):
    x = x_ref[0]
    ms = jnp.mean(x * x, axis=-1, keepdims=True)
    y = x * lax.rsqrt(ms + EPS) * g_ref[...]
    shift = mod_ref[0, 0:1, :]
    scale = mod_ref[0, 1:2, :]
    h = (y * (1.0 + scale) + shift).astype(BF16)
    o_a, o_gc, o_q, o_k, o_v, o_ga, o_qi, o_kw, o_gm = _seg_offsets(d_conv, d_attn)

    def mm(lo, width):
        return jnp.dot(h, w_ref[:, lo:lo + width], preferred_element_type=F32)

    a = mm(o_a, 2 * d_conv)
    u_ref[0] = a[:, :d_conv] * _sigmoid(a[:, d_conv:])
    sgc_ref[0] = _silu(mm(o_gc, d_conv)).astype(BF16)
    zqt = (mm(o_q, d_attn) * (HEAD_DIM ** -0.5 * LOG2E)).T
    for hh in range(N_HEADS):
        g, j = divmod(hh, GROUP)
        qt_ref[0, g, 0, :, j * tm:(j + 1) * tm] = (
            zqt[hh * HEAD_DIM:(hh + 1) * HEAD_DIM].astype(BF16))
    zk = mm(o_k, N_KV_HEADS * HEAD_DIM)
    zv = mm(o_v, N_KV_HEADS * HEAD_DIM)
    k_ref[0] = zk
    v_ref[0] = zv
    zvt = zv.T
    ones_rows = jnp.where(lax.broadcasted_iota(I32, (PACKED_ROWS, tm), 0) == 0, 1.0, 0.0)
    for g in range(N_KV_HEADS):
        kb_ref[0, g] = zk[:, g * HEAD_DIM:(g + 1) * HEAD_DIM].astype(BF16)
        vt_ref[0, g, 0, 0:HEAD_DIM] = zvt[g * HEAD_DIM:(g + 1) * HEAD_DIM].astype(BF16)
        vt_ref[0, g, 0, HEAD_DIM:] = ones_rows.astype(BF16)
    sga_ref[0] = _silu(mm(o_ga, d_attn)).astype(BF16)
    zqit = (mm(o_qi, IDX_HEADS * IDX_DIM) * (IDX_DIM ** -0.5)).T
    for hh in range(IDX_HEADS):
        qit_ref[0, 0, :, hh * tm:(hh + 1) * tm] = zqit[hh * IDX_DIM:(hh + 1) * IDX_DIM].astype(BF16)
    zkw = mm(o_kw, LANES)
    zki = zkw[:, :IDX_DIM]
    ki_ref[0] = zki
    kib_ref[0] = zki.astype(BF16)
    wi_ref[0] = zkw.T[IDX_DIM:IDX_DIM + IDX_HEADS] * (IDX_HEADS ** -0.5)
    gm_ref[0] = _sigmoid(mm(o_gm, 2 * d_model)).astype(BF16)


def _inproj(x, mod3, g_pre, w_pad, *, d_conv, d_attn):
    b, l, d = x.shape
    tm = min(256, l)
    nw = w_pad.shape[1]
    kvd = N_KV_HEADS * HEAD_DIM

    def row(width):
        return pl.BlockSpec((1, tm, width), lambda bi, i: (bi, i, 0))

    def heads(n, width):
        return pl.BlockSpec((1, n, tm, width), lambda bi, i: (bi, 0, i, 0))

    nq = l // tm
    vrows = HEAD_DIM + PACKED_ROWS
    out_shape = (
        jax.ShapeDtypeStruct((b, l, d_conv), F32),
        jax.ShapeDtypeStruct((b, l, d_conv), BF16),
        jax.ShapeDtypeStruct((b, N_KV_HEADS, nq, HEAD_DIM, GROUP * tm), BF16),
        jax.ShapeDtypeStruct((b, l, kvd), F32),
        jax.ShapeDtypeStruct((b, l, kvd), F32),
        jax.ShapeDtypeStruct((b, N_KV_HEADS, l, HEAD_DIM), BF16),
        jax.ShapeDtypeStruct((b, N_KV_HEADS, nq, vrows, tm), BF16),
        jax.ShapeDtypeStruct((b, l, d_attn), BF16),
        jax.ShapeDtypeStruct((b, nq, IDX_DIM, IDX_HEADS * tm), BF16),
        jax.ShapeDtypeStruct((b, l, IDX_DIM), F32),
        jax.ShapeDtypeStruct((b, l, IDX_DIM), BF16),
        jax.ShapeDtypeStruct((b, IDX_HEADS, l), F32),
        jax.ShapeDtypeStruct((b, l, 2 * d), BF16),
    )
    out_specs = (row(d_conv), row(d_conv),
                 pl.BlockSpec((1, N_KV_HEADS, 1, HEAD_DIM, GROUP * tm), lambda bi, i: (bi, 0, i, 0, 0)),
                 row(kvd), row(kvd), heads(N_KV_HEADS, HEAD_DIM),
                 pl.BlockSpec((1, N_KV_HEADS, 1, vrows, tm), lambda bi, i: (bi, 0, i, 0, 0)),
                 row(d_attn),
                 pl.BlockSpec((1, 1, IDX_DIM, IDX_HEADS * tm), lambda bi, i: (bi, i, 0, 0)),
                 row(IDX_DIM), row(IDX_DIM),
                 pl.BlockSpec((1, IDX_HEADS, tm), lambda bi, i: (bi, 0, i)),
                 row(2 * d))
    return pl.pallas_call(
        functools.partial(_inproj_kernel, tm=tm, d_conv=d_conv, d_attn=d_attn, d_model=d),
        grid=(b, l // tm),
        in_specs=[
            pl.BlockSpec((1, tm, d), lambda bi, i: (bi, i, 0)),
            pl.BlockSpec((1, 3, d), lambda bi, i: (bi, 0, 0)),
            pl.BlockSpec((1, d), lambda bi, i: (0, 0)),
            pl.BlockSpec((d, nw), lambda bi, i: (0, 0), pipeline_mode=pl.Buffered(1)),
        ],
        out_specs=out_specs,
        out_shape=out_shape,
        compiler_params=_params("arbitrary", "arbitrary"),
        name="inproj",
    )(x, mod3, g_pre, w_pad)


def _attn_kernel(qt_ref, qit_ref, wi_ref, k_ref, vt_ref, ki_ref, tb_ref, tri_ref, o_ref,
                 key_scr, hi_scr, lo_scr, mb_scr, m_scr, acc_scr, ta_scr, sa_scr, tb_scr, sb_scr,
                 *, tq, past, ksel):
    i = pl.program_id(1)
    q0 = past + i * tq
    nkb = (q0 + tq + KEY_BLOCK - 1) // KEY_BLOCK

    def key_rows(kb):
        return pl.ds(pl.multiple_of(kb * KEY_BLOCK, KEY_BLOCK), KEY_BLOCK)

    qit = qit_ref[0, 0]
    wi = wi_ref[0]
    last = nkb - 1
    last_ok = (lax.broadcasted_iota(I32, (KEY_BLOCK, tq), 0) // CHUNK
               <= lax.broadcasted_iota(I32, (KEY_BLOCK, tq), 1) // CHUNK)

    def score_block(kb, is_last):
        ki_blk = ki_ref[0, key_rows(kb), :]
        acc = jnp.zeros((KEY_BLOCK, tq), F32)
        for h in range(IDX_HEADS):
            s = jnp.dot(ki_blk, qit[:, h * tq:(h + 1) * tq], preferred_element_type=F32)
            acc = acc + jnp.maximum(s, 0.0) * wi[h:h + 1, :]
        sc = jnp.where(last_ok, acc, NEG_INF) if is_last else acc
        bits = pltpu.bitcast(sc, I32)
        key = bits ^ ((bits >> 31) & 0x7FFFFFFF)
        key_scr[kb] = key
        half = pl.ds(pl.multiple_of((kb % 2) * KEY_BLOCK, KEY_BLOCK), KEY_BLOCK)
        hi_scr[kb // 2, half, :] = (key >> 16).astype(I16)
        lo_scr[kb // 2, half, :] = ((key & 0xFFFF) + HALF_MIN).astype(I16)

    def over_blocks(fn, init=0):
        carry = lax.fori_loop(0, last, lambda kb, c: fn(kb, False, c), init)
        return fn(last, True, carry)

    def score_pair(kp, carry):
        score_block(2 * kp, False)
        score_block(2 * kp + 1, False)
        return carry

    lax.fori_loop(0, last // 2, score_pair, 0)

    @pl.when(last % 2 == 1)
    def _():
        score_block(last - 1, False)

    score_block(last, True)

    npair = (nkb + 1) // 2

    @pl.when(nkb % 2 == 1)
    def _():
        floor16 = jnp.full((KEY_BLOCK, tq), HALF_MIN, I16)
        hi_scr[npair - 1, KEY_BLOCK:, :] = floor16
        lo_scr[npair - 1, KEY_BLOCK:, :] = floor16

    def count16(scr, pred):
        groups = 2 * KEY_BLOCK // PACKED_ROWS

        def body(kp, acc):
            hit = jnp.where(pred(scr[kp]), jnp.int16(1), jnp.int16(0))
            hit = hit.reshape(groups, PACKED_ROWS, tq)
            parts = [hit[r] for r in range(groups)]
            while len(parts) > 1:
                parts = [parts[r] + parts[r + 1] for r in range(0, len(parts), 2)]
            return acc + parts[0]
        acc = lax.fori_loop(0, npair, body, jnp.zeros((PACKED_ROWS, tq), I16))
        return acc.astype(I32).sum(axis=0, keepdims=True)

    def search16(scr, target):
        def refine(it, thr):
            cand = thr + jnp.left_shift(jnp.int32(1), 15 - it)
            c16 = cand.astype(I16)
            return jnp.where(count16(scr, lambda x: x >= c16) >= target, cand, thr)
        return lax.fori_loop(0, 16, refine, jnp.full((1, tq), HALF_MIN, I32))

    thr_hi = search16(hi_scr, ksel)
    thr_hi16 = thr_hi.astype(I16)
    above = count16(hi_scr, lambda x: x > thr_hi16)

    def restrict(kp, carry):
        lo_scr[kp] = jnp.where(hi_scr[kp] == thr_hi16, lo_scr[kp], jnp.int16(HALF_MIN))
        return carry

    lax.fori_loop(0, npair, restrict, 0)
    thr_lo = search16(lo_scr, ksel - above)
    thr = thr_hi * 65536 + (thr_lo - HALF_MIN)

    thr_lo16 = thr_lo.astype(I16)
    n_ge = above + count16(lo_scr, lambda x: x >= thr_lo16)
    surplus = jnp.max(n_ge) > ksel

    @pl.when(jnp.logical_not(surplus))
    def _():
        def mask_block(kb, is_last, carry):
            keep = jnp.where(key_scr[kb] >= thr, 0.0, NEG_INF)
            mb_scr[kb] = (jnp.where(last_ok, keep, NEG_INF) if is_last else keep).astype(BF16)
            return carry

        over_blocks(mask_block)

    @pl.when(surplus)
    def _():
        def count_gt(kb, acc):
            hit = (key_scr[kb] > thr).astype(I32)
            return acc + hit.reshape(KEY_BLOCK // 8, 8, tq).sum(axis=0)

        n_gt = lax.fori_loop(0, nkb, count_gt, jnp.zeros((8, tq), I32)).sum(axis=0, keepdims=True)
        need = (ksel - n_gt).astype(F32)

        def mask_block(kb, is_last, ties_before):
            kk = key_scr[kb]
            eq = kk == thr
            rank = jnp.dot(tri_ref[...], jnp.where(eq, 1.0, 0.0).astype(BF16),
                           preferred_element_type=F32) + ties_before
            tie = jnp.where(eq, jnp.where(rank <= need, 0.0, NEG_INF), NEG_INF)
            keep = jnp.where(kk > thr, 0.0, tie)
            mb_scr[kb] = (jnp.where(last_ok, keep, NEG_INF) if is_last else keep).astype(BF16)
            return rank[KEY_BLOCK - 1:KEY_BLOCK, :]

        over_blocks(mask_block, jnp.zeros((1, tq), F32))

    nsub = tq // BIAS_TILE

    def bias_block(h, diag):
        same = tb_ref[h, 1]
        prev = tb_ref[h, 0]
        far = jnp.broadcast_to(tb_ref[h, 0, 0:1, 0:1], (BIAS_TILE, BIAS_TILE))
        if diag:
            grid = [[same, prev], [far, same]]
        else:
            grid = [[far, far], [prev, far]]
        return jnp.concatenate([jnp.concatenate(r[:nsub], axis=1) for r in grid], axis=0)

    def produce(h, kb, kind, buf, gated=False):
        t_buf, s_buf = buf
        g, j = divmod(h, GROUP)
        lanes = slice(j * tq, (j + 1) * tq)
        t = jnp.dot(k_ref[0, g, key_rows(kb), :], qt_ref[0, g, 0, :, lanes],
                    preferred_element_type=F32).astype(BF16) + mb_scr[kb]
        if kind == "far":
            shift = jnp.broadcast_to(tb_ref[h, 0, 0:1, 0:1], (1, tq))
        else:
            shift = jnp.zeros((1, tq), F32)
            gate = jnp.where(nkb >= 2, 0.0, NEG_INF) if gated else 0.0
            t = t + (bias_block(h, kind == "diag") + gate).astype(BF16)
        t_buf[g, :, lanes] = t
        s_buf[g, 0:1, lanes] = jnp.max(t, axis=0, keepdims=True).astype(F32) + shift
        s_buf[g, 1:2, lanes] = shift

    def consume(h, kb, buf):
        t_buf, s_buf = buf
        g, j = divmod(h, GROUP)
        lanes = slice(j * tq, (j + 1) * tq)
        m = m_scr[g, :, lanes]
        shift = s_buf[g, 1:2, lanes]
        m_new = jnp.maximum(m, s_buf[g, 0:1, lanes])
        sub = (jnp.where(m_new == NEG_INF, 0.0, m_new) - shift).astype(BF16)
        m_used = sub.astype(F32) + shift
        p = jnp.exp2(t_buf[g, :, lanes] - sub)
        acc_scr[g, :, lanes] = jnp.exp2(m - m_used) * acc_scr[g, :, lanes] + jnp.dot(
            vt_ref[0, g, kb], p, preferred_element_type=F32)
        m_scr[g, :, lanes] = jnp.where(m_new == NEG_INF, NEG_INF, m_used)

    def stage(new=None, old=None):
        for h in range(N_HEADS):
            if new is not None:
                produce(h, *new)
            if old is not None:
                consume(h, *old)

    m_scr[...] = jnp.full(m_scr.shape, NEG_INF, F32)
    acc_scr[...] = jnp.zeros(acc_scr.shape, F32)

    buf_a, buf_b = (ta_scr, sa_scr), (tb_scr, sb_scr)
    n_far = jnp.maximum(nkb - 2, 0)
    off = jnp.maximum(nkb - 2, 0)
    stage(new=(last, "diag", buf_a))
    stage(new=(off, "off", buf_b, True), old=(last, buf_a))

    def far_pair(it, carry):
        f = nkb - 3 - 2 * it
        stage(new=(f, "far", buf_a), old=(f + 1, buf_b))
        stage(new=(f - 1, "far", buf_b), old=(f, buf_a))
        return carry

    lax.fori_loop(0, n_far // 2, far_pair, 0)

    @pl.when(n_far % 2 == 1)
    def _():
        stage(new=(0, "far", buf_a), old=(1, buf_b))
        stage(old=(0, buf_a))

    @pl.when(n_far % 2 == 0)
    def _():
        stage(old=(jnp.maximum(nkb - 2 - n_far, 0), buf_b))

    heads_t = []
    for g in range(N_KV_HEADS):
        acc = acc_scr[g]
        o = acc[:HEAD_DIM] / acc[HEAD_DIM:HEAD_DIM + 1]
        heads_t += [o[:, j * tq:(j + 1) * tq] for j in range(GROUP)]
    o_ref[0] = jnp.concatenate(heads_t, axis=0).T.astype(o_ref.dtype)


def _attention(qt, qit, wi_t, k_hm, vt, kib, tb, tri, *, tq, past, ksel):
    b, _, nq, _, _ = qt.shape
    lk = k_hm.shape[2]
    nkb_max = lk // KEY_BLOCK
    blocks = (nkb_max, KEY_BLOCK, tq)
    pairs = ((nkb_max + 1) // 2, 2 * KEY_BLOCK, tq)
    return pl.pallas_call(
        functools.partial(_attn_kernel, tq=tq, past=past, ksel=ksel),
        grid=(b, nq),
        in_specs=[
            pl.BlockSpec((1, N_KV_HEADS, 1, HEAD_DIM, GROUP * tq), lambda bi, i: (bi, 0, i, 0, 0)),
            pl.BlockSpec((1, 1, IDX_DIM, IDX_HEADS * tq), lambda bi, i: (bi, i, 0, 0)),
            pl.BlockSpec((1, IDX_HEADS, tq), lambda bi, i: (bi, 0, i)),
            pl.BlockSpec((1, N_KV_HEADS, lk, HEAD_DIM), lambda bi, i: (bi, 0, 0, 0)),
            pl.BlockSpec((1, N_KV_HEADS, nkb_max, vt.shape[3], KEY_BLOCK),
                         lambda bi, i: (bi, 0, 0, 0, 0)),
            pl.BlockSpec((1, lk, IDX_DIM), lambda bi, i: (bi, 0, 0)),
            pl.BlockSpec((N_HEADS, 2, BIAS_TILE, BIAS_TILE), lambda bi, i: (0, 0, 0, 0)),
            pl.BlockSpec((KEY_BLOCK, KEY_BLOCK), lambda bi, i: (0, 0)),
        ],
        out_specs=pl.BlockSpec((1, tq, N_HEADS * HEAD_DIM), lambda bi, i: (bi, i, 0)),
        out_shape=jax.ShapeDtypeStruct((b, nq * tq, N_HEADS * HEAD_DIM), BF16),
        scratch_shapes=[pltpu.VMEM(blocks, I32), pltpu.VMEM(pairs, I16), pltpu.VMEM(pairs, I16),
                        pltpu.VMEM(blocks, BF16),
                        pltpu.VMEM((N_KV_HEADS, 1, GROUP * tq), F32),
                        pltpu.VMEM((N_KV_HEADS, vt.shape[3], GROUP * tq), F32)]
        + 2 * [pltpu.VMEM((N_KV_HEADS, KEY_BLOCK, GROUP * tq), BF16),
               pltpu.VMEM((N_KV_HEADS, 2, GROUP * tq), F32)],
        compiler_params=_params("arbitrary", "arbitrary"),
        name="attention",
    )(qt, qit, wi_t, k_hm, vt, kib, tb, tri)


def _post_kernel(x_ref, mod_ref, u_ref, halo_ref, prev_ref, sgc_ref, o_ref, sga_ref, gm_ref,
                 cw_ref, cb_ref, lg_ref, lb_ref, wc_ref, wa_ref, wo_ref, gp_ref,
                 y_ref, up_scr, sh_scr, cv_scr, *, tt, d_conv, d_model):
    i = pl.program_id(1)
    up_scr[0:CONV_HALO] = jnp.where(i == 0, prev_ref[0], halo_ref[0])
    up_scr[CONV_HALO:] = u_ref[0]
    span = tt + CONV_HALO - SUBLANES
    for s in range(1, SUBLANES):
        sh_scr[s - 1] = up_scr[pl.ds(s, span), :]
    pad = CONV_HALO - (CONV_W - 1)
    rt = min(tt, 128)
    for r0 in range(0, tt, rt):
        for c0 in range(0, d_conv, LANES):
            acc = jnp.broadcast_to(cb_ref[:, c0:c0 + LANES], (rt, LANES))
            for j in range(CONV_W):
                a, s = divmod(pad + j, SUBLANES)
                rows = pl.ds(r0 + SUBLANES * a, rt)
                win = up_scr[rows, c0:c0 + LANES] if s == 0 else sh_scr[s - 1, rows, c0:c0 + LANES]
                acc = acc + cw_ref[j:j + 1, c0:c0 + LANES] * win
            cv_scr[r0:r0 + rt, c0:c0 + LANES] = acc
    cv = cv_scr[...]
    mu = jnp.mean(cv, axis=-1, keepdims=True)
    dv = cv - mu
    var = jnp.mean(dv * dv, axis=-1, keepdims=True)
    yc = _silu(dv * lax.rsqrt(var + EPS) * lg_ref[...] + lb_ref[...])
    yc = (yc * sgc_ref[0].astype(F32)).astype(BF16)
    y_c = jnp.dot(yc, wc_ref[...], preferred_element_type=F32)
    ya = (o_ref[0].astype(F32) * sga_ref[0].astype(F32)).astype(BF16)
    y_a = jnp.dot(ya, wa_ref[...], preferred_element_type=F32)
    gm = gm_ref[0].astype(F32)
    mix = (gm[:, :d_model] * y_c + gm[:, d_model:] * y_a).astype(BF16)
    y = jnp.dot(mix, wo_ref[...], preferred_element_type=F32)
    y = y * lax.rsqrt(jnp.mean(y * y, axis=-1, keepdims=True) + EPS) * gp_ref[...]
    y_ref[0] = x_ref[0] + mod_ref[0, 2:3, :] * y


def _post(x, mod3, u, prev, sgc, o, sga, gm, conv_w, conv_b, ln_g, ln_b, wc, wa, wo, g_post):
    b, l, d = x.shape
    d_conv = u.shape[2]
    d_attn = o.shape[2]
    tt = min(256, l)
    hb = tt // CONV_HALO

    def row(width):
        return pl.BlockSpec((1, tt, width), lambda bi, i: (bi, i, 0))

    def const(shape):
        return pl.BlockSpec(shape, lambda bi, i: (0,) * len(shape))

    return pl.pallas_call(
        functools.partial(_post_kernel, tt=tt, d_conv=d_conv, d_model=d),
        grid=(b, l // tt),
        in_specs=[
            row(d),
            pl.BlockSpec((1, 3, d), lambda bi, i: (bi, 0, 0)),
            row(d_conv),
            pl.BlockSpec((1, CONV_HALO, d_conv), lambda bi, i: (bi, jnp.maximum(i * hb - 1, 0), 0)),
            pl.BlockSpec((1, CONV_HALO, d_conv), lambda bi, i: (bi, 0, 0)),
            row(d_conv), row(d_attn), row(d_attn), row(2 * d),
            const((CONV_W, d_conv)), const((1, d_conv)), const((1, d_conv)), const((1, d_conv)),
            const((d_conv, d)), const((d_attn, d)), const((d, d)), const((1, d)),
        ],
        out_specs=row(d),
        out_shape=jax.ShapeDtypeStruct((b, l, d), F32),
        scratch_shapes=[pltpu.VMEM((tt + CONV_HALO, d_conv), F32),
                        pltpu.VMEM((SUBLANES - 1, tt + CONV_HALO - SUBLANES, d_conv), F32),
                        pltpu.VMEM((tt, d_conv), F32)],
        compiler_params=_params("arbitrary", "arbitrary"),
        name="post",
    )(x, mod3, u, u, prev, sgc, o, sga, gm, conv_w, conv_b, ln_g, ln_b, wc, wa, wo, g_post)


def _pad_axis(t, axis, size):
    if t.shape[axis] == size:
        return t
    widths = [(0, 0)] * t.ndim
    widths[axis] = (0, size - t.shape[axis])
    return jnp.pad(t, widths)


def _sublayer(x, l_true, mod3, conv_prev, past, tb, tri, w):
    b, l, d = x.shape
    d_conv = w["conv_w"].shape[1]
    d_attn = N_HEADS * HEAD_DIM
    (u, sgc, qt, k, v, k_hm, vt, sga, qit, ki, kib, wi_t, gm) = _inproj(
        x, mod3, w["g_pre"], w["w_in"], d_conv=d_conv, d_attn=d_attn)

    if past is None:
        p_len = 0
    else:
        ck, cv, cki = past
        p_len = ck.shape[1]
        k_hm = jnp.concatenate([ck.transpose(0, 2, 1, 3).astype(BF16), k_hm], axis=2)
        ones_rows = jnp.zeros((b, N_KV_HEADS, PACKED_ROWS, p_len), BF16).at[:, :, 0].set(1.0)
        cvt = jnp.concatenate([cv.transpose(0, 2, 3, 1).astype(BF16), ones_rows], axis=2)
        cvt = cvt.reshape(b, N_KV_HEADS, cvt.shape[2], p_len // KEY_BLOCK, KEY_BLOCK)
        vt = jnp.concatenate([cvt.transpose(0, 1, 3, 2, 4), _pad_axis(vt, 4, KEY_BLOCK)], axis=2)
        kib = jnp.concatenate([cki.astype(BF16), kib], axis=1)
    ksel = min(TOPK_MAX, (p_len + l_true) // 4)
    tq = min(KEY_BLOCK, l)
    lk_pad = -(-(p_len + l) // KEY_BLOCK) * KEY_BLOCK
    assert p_len % KEY_BLOCK == 0 and (tq == KEY_BLOCK or l == tq)
    assert (p_len + l_true) % CHUNK == 0 and ksel <= KEY_BLOCK
    o = _attention(qt, qit, wi_t, _pad_axis(k_hm, 2, lk_pad), vt, _pad_axis(kib, 1, lk_pad),
                   tb, tri, tq=tq, past=p_len, ksel=ksel)

    prev = jnp.pad(conv_prev, ((0, 0), (CONV_HALO - (CONV_W - 1), 0), (0, 0)))
    y = _post(x, mod3, u, prev, sgc, o, sga, gm, w["conv_w"], w["conv_b"], w["ln_g"], w["ln_b"],
              w["w_conv_out"], w["w_attn_out"], w["w_out"], w["g_post"])
    conv_new = jnp.concatenate([conv_prev, u[:, :l_true]], axis=1)[:, -(CONV_W - 1):]
    return (y, conv_new, k[:, :l_true].reshape(b, l_true, N_KV_HEADS, HEAD_DIM),
            v[:, :l_true].reshape(b, l_true, N_KV_HEADS, HEAD_DIM), ki[:, :l_true])


def _pack_w_in(w_in, d_conv, d_attn, d_model):
    kvd = N_KV_HEADS * HEAD_DIM
    splits = (2 * d_conv, d_conv, d_attn, kvd, kvd, d_attn, IDX_HEADS * IDX_DIM, IDX_DIM,
              IDX_HEADS, 2 * d_model)
    offs = [0]
    for s in splits:
        offs.append(offs[-1] + s)
    seg = [w_in[:, offs[j]:offs[j + 1]] for j in range(len(splits))]
    kw = jnp.concatenate([seg[7], seg[8]], axis=1)
    kw = jnp.pad(kw, ((0, 0), (0, LANES - kw.shape[1])))
    return jnp.concatenate(seg[:7] + [kw, seg[9]], axis=1).astype(BF16)


def kernel(x_prompt, x_sample, c_prompt, c_sample, cache_k, cache_v, cache_kidx, state_conv,
           rel_bias, ada_w, ada_b, norm_pre, norm_post, w_in, conv_w, conv_b, conv_ln_g,
           conv_ln_b, w_conv_out, w_attn_out, w_out):
    depth = ada_w.shape[0]
    bp, _, d = x_prompt.shape
    d_conv = conv_w.shape[2]
    d_attn = N_HEADS * HEAD_DIM
    mod = _ada(jnp.concatenate([c_prompt, c_sample], axis=0), ada_w, ada_b)
    mod = mod.reshape(depth, mod.shape[1], 3, d)
    tb = _bias_tiles(rel_bias)
    tri = (jnp.arange(KEY_BLOCK)[:, None] >= jnp.arange(KEY_BLOCK)[None, :]).astype(BF16)

    lp, ls = x_prompt.shape[1], x_sample.shape[1]
    xp, xs = x_prompt, _pad_axis(x_sample, 1, -(-ls // LANES) * LANES)
    outs = [[] for _ in range(8)]
    for l in range(depth):
        w = dict(
            g_pre=norm_pre[l][None], g_post=norm_post[l][None],
            w_in=_pack_w_in(w_in[l], d_conv, d_attn, d),
            conv_w=conv_w[l], conv_b=conv_b[l][None], ln_g=conv_ln_g[l][None],
            ln_b=conv_ln_b[l][None], w_conv_out=w_conv_out[l].astype(BF16),
            w_attn_out=w_attn_out[l].astype(BF16), w_out=w_out[l].astype(BF16))
        zeros_conv = jnp.zeros((bp, CONV_W - 1, d_conv), xp.dtype)
        xp, cp, kp, vp, kip = _sublayer(xp, lp, mod[l, :bp], zeros_conv, None, tb, tri, w)
        xs, cs, kss, vss, kis = _sublayer(xs, ls, mod[l, bp:], state_conv[l],
                                          (cache_k[l], cache_v[l], cache_kidx[l]), tb, tri, w)
        for lst, val in zip(outs, (kp, vp, kip, cp, kss, vss, kis, cs)):
            lst.append(val)
    return (xp, xs[:, :ls]) + tuple(jnp.stack(o) for o in outs)
```

```python
import functools
import math

import jax
import jax.numpy as jnp
from jax import lax
from jax.experimental import pallas as pl
from jax.experimental.pallas import tpu as pltpu

F32 = jnp.float32
BF16 = jnp.bfloat16
I32 = jnp.int32
I16 = jnp.int16

CHUNK = 64
CONV_W = 31
N_HEADS = 16
N_KV_HEADS = 4
HEAD_DIM = 64
GROUP = N_HEADS // N_KV_HEADS
IDX_HEADS = 8
IDX_DIM = 64
TOPK_MAX = 256
N_BUCKETS = 32
MAX_DIST = 128
EPS = 1e-6

LANES = 128
SUBLANES = 8
PACKED_ROWS = 16
KEY_BLOCK = 256
BIAS_TILE = 128
CONV_HALO = 32
PIECE = 256
VMEM_LIMIT_BYTES = 56 * 1024 * 1024
NEG_INF = float("-inf")
INT_MIN = -2 ** 31
HALF_MIN = -2 ** 15
LOG2E = 1.4426950408889634


def _sigmoid(x):
    return 1.0 / (1.0 + jnp.exp(-x))


def _silu(x):
    return x * _sigmoid(x)


def _params(*sem):
    return pltpu.CompilerParams(dimension_semantics=sem, vmem_limit_bytes=VMEM_LIMIT_BYTES)


def _ada_kernel(c_ref, w_ref, b_ref, o_ref):
    c = c_ref[...]
    o_ref[0] = jnp.dot(_silu(c), w_ref[0], preferred_element_type=F32) + b_ref[0]


def _ada(c_all, ada_w, ada_b):
    depth, d, _ = ada_w.shape
    bc = c_all.shape[0]
    return pl.pallas_call(
        _ada_kernel,
        grid=(depth, 3),
        in_specs=[
            pl.BlockSpec((bc, d), lambda l, n: (0, 0)),
            pl.BlockSpec((1, d, d), lambda l, n: (l, 0, n)),
            pl.BlockSpec((1, 1, d), lambda l, n: (l, 0, n)),
        ],
        out_specs=pl.BlockSpec((1, bc, d), lambda l, n: (l, 0, n)),
        out_shape=jax.ShapeDtypeStruct((depth, bc, 3 * d), F32),
        compiler_params=_params("arbitrary", "arbitrary"),
        name="ada",
    )(c_all, ada_w, ada_b.reshape(depth, 1, 3 * d))


def _bias_kernel(rb_ref, o_ref):
    r = lax.broadcasted_iota(I32, (BIAS_TILE, BIAS_TILE), 0)
    c = lax.broadcasted_iota(I32, (BIAS_TILE, BIAS_TILE), 1)
    nb = N_BUCKETS // 2
    max_exact = nb // 2
    for t in range(2):
        rel = (t - 1) * BIAS_TILE + r - c
        ret = jnp.where(rel > 0, nb, 0)
        n = jnp.abs(rel)
        nf = jnp.maximum(n, 1).astype(F32)
        large = max_exact + (jnp.log(nf / max_exact) / math.log(MAX_DIST / max_exact)
                             * (nb - max_exact)).astype(I32)
        large = jnp.minimum(large, nb - 1)
        bucket = ret + jnp.where(n < max_exact, n, large)
        for h in range(N_HEADS):
            acc = jnp.zeros((BIAS_TILE, BIAS_TILE), F32)
            for b in range(N_BUCKETS):
                acc = jnp.where(bucket == b, rb_ref[b, h], acc)
            o_ref[h, t] = acc * LOG2E


def _bias_tiles(rel_bias):
    return pl.pallas_call(
        _bias_kernel,
        in_specs=[pl.BlockSpec(memory_space=pltpu.SMEM)],
        out_specs=pl.BlockSpec(memory_space=pltpu.VMEM),
        out_shape=jax.ShapeDtypeStruct((N_HEADS, 2, BIAS_TILE, BIAS_TILE), F32),
        name="bias_tiles",
    )(rel_bias)


def _seg_offsets(d_conv, d_attn):
    widths = (2 * d_conv, d_conv, d_attn, N_KV_HEADS * HEAD_DIM, N_KV_HEADS * HEAD_DIM, d_attn,
              IDX_HEADS * IDX_DIM, LANES, None)
    offs, acc = [], 0
    for w in widths:
        offs.append(acc)
        if w is not None:
            acc += w
    return offs


def _causal_conv(u, prev, is_first, cw_ref, cb_ref, out_ref, up_scr, sh_scr, *, tt, d_conv):
    up_scr[0:CONV_HALO] = jnp.where(is_first, prev, up_scr[tt:tt + CONV_HALO])
    up_scr[CONV_HALO:] = u
    span = tt + CONV_HALO - SUBLANES
    for s in range(1, SUBLANES):
        sh_scr[s - 1] = up_scr[pl.ds(s, span), :]
    pad = CONV_HALO - (CONV_W - 1)
    rt = min(tt, 64)

    def taps(r0, c0, anchor):
        acc = jnp.broadcast_to(cb_ref[:, c0:c0 + LANES] + 0.0 * anchor, (rt, LANES))
        for j in range(CONV_W):
            a, s = divmod(pad + j, SUBLANES)
            rows = pl.ds(r0 + SUBLANES * a, rt)
            win = up_scr[rows, c0:c0 + LANES] if s == 0 else sh_scr[s - 1, rows, c0:c0 + LANES]
            acc = acc + cw_ref[j:j + 1, c0:c0 + LANES] * win
        out_ref[0, r0:r0 + rt, c0:c0 + LANES] = acc

    return [functools.partial(taps, r0, c0)
            for r0 in range(0, tt, rt) for c0 in range(0, d_conv, LANES)]


def _inproj_kernel(x_ref, mod_ref, g_ref, w_ref, prev_ref, cw_ref, cb_ref,
                   u_ref, cv_ref, sgc_ref, qt_ref, k_ref, v_ref, kb_ref, vt_ref, sga_ref, qit_ref,
                   ki_ref, kib_ref, wi_ref, gm_ref, up_scr, sh_scr,
                   *, tm, d_conv, d_attn, d_model):
    x = x_ref[0]
    ms = jnp.mean(x * x, axis=-1, keepdims=True)
    y = x * lax.rsqrt(ms + EPS) * g_ref[...]
    shift = mod_ref[0, 0:1, :]
    scale = mod_ref[0, 1:2, :]
    h = (y * (1.0 + scale) + shift).astype(BF16)
    o_a, o_gc, o_q, o_k, o_v, o_ga, o_qi, o_kw, o_gm = _seg_offsets(d_conv, d_attn)

    anchor = [None]

    def mm(lo, width):
        z = jnp.dot(h, w_ref[:, lo:lo + width], preferred_element_type=F32)
        anchor[0] = z[0:1, 0:LANES]
        return z

    a = mm(o_a, 2 * d_conv)
    u = a[:, :d_conv] * _sigmoid(a[:, d_conv:])
    u_ref[0] = u
    conv_chunks = _causal_conv(u, prev_ref[0], pl.program_id(1) == 0, cw_ref, cb_ref, cv_ref,
                               up_scr, sh_scr, tt=tm, d_conv=d_conv)
    nw = w_ref.shape[1]
    cols_left = [nw - o_gc]
    n_chunks = len(conv_chunks)

    def after(width):
        cols_left[0] -= width
        due = n_chunks - (n_chunks * cols_left[0]) // (nw - o_gc)
        while len(conv_chunks) > n_chunks - due:
            conv_chunks.pop(0)(anchor[0])

    for c in range(0, d_conv, PIECE):
        sgc_ref[0, :, c:c + PIECE] = _silu(mm(o_gc + c, PIECE)).astype(BF16)
        after(PIECE)
    for c in range(0, d_attn, PIECE):
        zqt = (mm(o_q + c, PIECE) * (HEAD_DIM ** -0.5 * LOG2E)).T
        for hp in range(PIECE // HEAD_DIM):
            g, j = divmod(c // HEAD_DIM + hp, GROUP)
            qt_ref[0, g, 0, :, j * tm:(j + 1) * tm] = (
                zqt[hp * HEAD_DIM:(hp + 1) * HEAD_DIM].astype(BF16))
        after(PIECE)
    zk = mm(o_k, N_KV_HEADS * HEAD_DIM)
    zv = mm(o_v, N_KV_HEADS * HEAD_DIM)
    k_ref[0] = zk
    v_ref[0] = zv
    zvt = zv.T
    ones_rows = jnp.where(lax.broadcasted_iota(I32, (PACKED_ROWS, tm), 0) == 0, 1.0, 0.0)
    for g in range(N_KV_HEADS):
        kb_ref[0, g] = zk[:, g * HEAD_DIM:(g + 1) * HEAD_DIM].astype(BF16)
        vt_ref[0, g, 0, 0:HEAD_DIM] = zvt[g * HEAD_DIM:(g + 1) * HEAD_DIM].astype(BF16)
        vt_ref[0, g, 0, HEAD_DIM:] = ones_rows.astype(BF16)
    after(2 * N_KV_HEADS * HEAD_DIM)
    for c in range(0, d_attn, PIECE):
        sga_ref[0, :, c:c + PIECE] = _silu(mm(o_ga + c, PIECE)).astype(BF16)
        after(PIECE)
    zqit = (mm(o_qi, IDX_HEADS * IDX_DIM) * (IDX_DIM ** -0.5)).T
    for hh in range(IDX_HEADS):
        qit_ref[0, 0, :, hh * tm:(hh + 1) * tm] = zqit[hh * IDX_DIM:(hh + 1) * IDX_DIM].astype(BF16)
    after(IDX_HEADS * IDX_DIM)
    zkw = mm(o_kw, LANES)
    zki = zkw[:, :IDX_DIM]
    ki_ref[0] = zki
    kib_ref[0] = zki.astype(BF16)
    wi_ref[0] = zkw.T[IDX_DIM:IDX_DIM + IDX_HEADS] * (IDX_HEADS ** -0.5)
    after(LANES)
    for c in range(0, 2 * d_model, PIECE):
        gm_ref[0, :, c:c + PIECE] = _sigmoid(mm(o_gm + c, PIECE)).astype(BF16)
        after(PIECE)


def _inproj(x, mod3, g_pre, w_pad, prev, conv_w, conv_b, *, d_conv, d_attn):
    b, l, d = x.shape
    tm = min(256, l)
    nw = w_pad.shape[1]
    kvd = N_KV_HEADS * HEAD_DIM

    def row(width):
        return pl.BlockSpec((1, tm, width), lambda bi, i: (bi, i, 0))

    def heads(n, width):
        return pl.BlockSpec((1, n, tm, width), lambda bi, i: (bi, 0, i, 0))

    nq = l // tm
    vrows = HEAD_DIM + PACKED_ROWS
    out_shape = (
        jax.ShapeDtypeStruct((b, l, d_conv), F32),
        jax.ShapeDtypeStruct((b, l, d_conv), F32),
        jax.ShapeDtypeStruct((b, l, d_conv), BF16),
        jax.ShapeDtypeStruct((b, N_KV_HEADS, nq, HEAD_DIM, GROUP * tm), BF16),
        jax.ShapeDtypeStruct((b, l, kvd), F32),
        jax.ShapeDtypeStruct((b, l, kvd), F32),
        jax.ShapeDtypeStruct((b, N_KV_HEADS, l, HEAD_DIM), BF16),
        jax.ShapeDtypeStruct((b, N_KV_HEADS, nq, vrows, tm), BF16),
        jax.ShapeDtypeStruct((b, l, d_attn), BF16),
        jax.ShapeDtypeStruct((b, nq, IDX_DIM, IDX_HEADS * tm), BF16),
        jax.ShapeDtypeStruct((b, l, IDX_DIM), F32),
        jax.ShapeDtypeStruct((b, l, IDX_DIM), BF16),
        jax.ShapeDtypeStruct((b, IDX_HEADS, l), F32),
        jax.ShapeDtypeStruct((b, l, 2 * d), BF16),
    )
    out_specs = (row(d_conv), row(d_conv), row(d_conv),
                 pl.BlockSpec((1, N_KV_HEADS, 1, HEAD_DIM, GROUP * tm), lambda bi, i: (bi, 0, i, 0, 0)),
                 row(kvd), row(kvd), heads(N_KV_HEADS, HEAD_DIM),
                 pl.BlockSpec((1, N_KV_HEADS, 1, vrows, tm), lambda bi, i: (bi, 0, i, 0, 0)),
                 row(d_attn),
                 pl.BlockSpec((1, 1, IDX_DIM, IDX_HEADS * tm), lambda bi, i: (bi, i, 0, 0)),
                 row(IDX_DIM), row(IDX_DIM),
                 pl.BlockSpec((1, IDX_HEADS, tm), lambda bi, i: (bi, 0, i)),
                 row(2 * d))
    return pl.pallas_call(
        functools.partial(_inproj_kernel, tm=tm, d_conv=d_conv, d_attn=d_attn, d_model=d),
        grid=(b, l // tm),
        in_specs=[
            pl.BlockSpec((1, tm, d), lambda bi, i: (bi, i, 0)),
            pl.BlockSpec((1, 3, d), lambda bi, i: (bi, 0, 0)),
            pl.BlockSpec((1, d), lambda bi, i: (0, 0)),
            pl.BlockSpec((d, nw), lambda bi, i: (0, 0), pipeline_mode=pl.Buffered(1)),
            pl.BlockSpec((1, CONV_HALO, d_conv), lambda bi, i: (bi, 0, 0)),
            pl.BlockSpec((CONV_W, d_conv), lambda bi, i: (0, 0)),
            pl.BlockSpec((1, d_conv), lambda bi, i: (0, 0)),
        ],
        out_specs=out_specs,
        out_shape=out_shape,
        scratch_shapes=[pltpu.VMEM((tm + CONV_HALO, d_conv), F32),
                        pltpu.VMEM((SUBLANES - 1, tm + CONV_HALO - SUBLANES, d_conv), F32)],
        compiler_params=_params("arbitrary", "arbitrary"),
        name="inproj",
    )(x, mod3, g_pre, w_pad, prev, conv_w, conv_b)


def _attn_kernel(qt_ref, qit_ref, wi_ref, k_ref, vt_ref, ki_ref, tb_ref, tri_ref, o_ref,
                 key_scr, hi_scr, lo_scr, mb_scr, m_scr, acc_scr, ta_scr, sa_scr, tb_scr, sb_scr,
                 *, tq, past, ksel):
    i = pl.program_id(1)
    q0 = past + i * tq
    nkb = (q0 + tq + KEY_BLOCK - 1) // KEY_BLOCK

    def key_rows(kb):
        return pl.ds(pl.multiple_of(kb * KEY_BLOCK, KEY_BLOCK), KEY_BLOCK)

    qit = qit_ref[0, 0]
    wi = wi_ref[0]
    last = nkb - 1
    last_ok = (lax.broadcasted_iota(I32, (KEY_BLOCK, tq), 0) // CHUNK
               <= lax.broadcasted_iota(I32, (KEY_BLOCK, tq), 1) // CHUNK)

    def score_block(kb, is_last):
        ki_blk = ki_ref[0, key_rows(kb), :]
        acc = jnp.zeros((KEY_BLOCK, tq), F32)
        for h in range(IDX_HEADS):
            s = jnp.dot(ki_blk, qit[:, h * tq:(h + 1) * tq], preferred_element_type=F32)
            acc = acc + jnp.maximum(s, 0.0) * wi[h:h + 1, :]
        sc = jnp.where(last_ok, acc, NEG_INF) if is_last else acc
        bits = pltpu.bitcast(sc, I32)
        key = bits ^ ((bits >> 31) & 0x7FFFFFFF)
        key_scr[kb] = key
        half = pl.ds(pl.multiple_of((kb % 2) * KEY_BLOCK, KEY_BLOCK), KEY_BLOCK)
        hi_scr[kb // 2, half, :] = (key >> 16).astype(I16)
        lo_scr[kb // 2, half, :] = ((key & 0xFFFF) + HALF_MIN).astype(I16)

    def over_blocks(fn, init=0):
        carry = lax.fori_loop(0, last, lambda kb, c: fn(kb, False, c), init)
        return fn(last, True, carry)

    def score_pair(kp, carry):
        score_block(2 * kp, False)
        score_block(2 * kp + 1, False)
        return carry

    lax.fori_loop(0, last // 2, score_pair, 0)

    @pl.when(last % 2 == 1)
    def _():
        score_block(last - 1, False)

    score_block(last, True)

    npair = (nkb + 1) // 2

    @pl.when(nkb % 2 == 1)
    def _():
        floor16 = jnp.full((KEY_BLOCK, tq), HALF_MIN, I16)
        hi_scr[npair - 1, KEY_BLOCK:, :] = floor16
        lo_scr[npair - 1, KEY_BLOCK:, :] = floor16

    def count16(scr, pred):
        groups = 2 * KEY_BLOCK // PACKED_ROWS

        def body(kp, acc):
            hit = jnp.where(pred(scr[kp]), jnp.int16(1), jnp.int16(0))
            hit = hit.reshape(groups, PACKED_ROWS, tq)
            parts = [hit[r] for r in range(groups)]
            while len(parts) > 1:
                parts = [parts[r] + parts[r + 1] for r in range(0, len(parts), 2)]
            return acc + parts[0]
        acc = lax.fori_loop(0, npair, body, jnp.zeros((PACKED_ROWS, tq), I16))
        return acc.astype(I32).sum(axis=0, keepdims=True)

    def search16(scr, target):
        def refine(it, thr):
            cand = thr + jnp.left_shift(jnp.int32(1), 15 - it)
            c16 = cand.astype(I16)
            return jnp.where(count16(scr, lambda x: x >= c16) >= target, cand, thr)
        return lax.fori_loop(0, 16, refine, jnp.full((1, tq), HALF_MIN, I32))

    thr_hi = search16(hi_scr, ksel)
    thr_hi16 = thr_hi.astype(I16)
    above = count16(hi_scr, lambda x: x > thr_hi16)

    def restrict(kp, carry):
        lo_scr[kp] = jnp.where(hi_scr[kp] == thr_hi16, lo_scr[kp], jnp.int16(HALF_MIN))
        return carry

    lax.fori_loop(0, npair, restrict, 0)
    thr_lo = search16(lo_scr, ksel - above)
    thr = thr_hi * 65536 + (thr_lo - HALF_MIN)

    thr_lo16 = thr_lo.astype(I16)
    n_ge = above + count16(lo_scr, lambda x: x >= thr_lo16)
    surplus = jnp.max(n_ge) > ksel

    @pl.when(jnp.logical_not(surplus))
    def _():
        def mask_block(kb, is_last, carry):
            keep = jnp.where(key_scr[kb] >= thr, 0.0, NEG_INF)
            mb_scr[kb] = (jnp.where(last_ok, keep, NEG_INF) if is_last else keep).astype(BF16)
            return carry

        over_blocks(mask_block)

    @pl.when(surplus)
    def _():
        def count_gt(kb, acc):
            hit = (key_scr[kb] > thr).astype(I32)
            return acc + hit.reshape(KEY_BLOCK // 8, 8, tq).sum(axis=0)

        n_gt = lax.fori_loop(0, nkb, count_gt, jnp.zeros((8, tq), I32)).sum(axis=0, keepdims=True)
        need = (ksel - n_gt).astype(F32)

        def mask_block(kb, is_last, ties_before):
            kk = key_scr[kb]
            eq = kk == thr
            rank = jnp.dot(tri_ref[...], jnp.where(eq, 1.0, 0.0).astype(BF16),
                           preferred_element_type=F32) + ties_before
            tie = jnp.where(eq, jnp.where(rank <= need, 0.0, NEG_INF), NEG_INF)
            keep = jnp.where(kk > thr, 0.0, tie)
            mb_scr[kb] = (jnp.where(last_ok, keep, NEG_INF) if is_last else keep).astype(BF16)
            return rank[KEY_BLOCK - 1:KEY_BLOCK, :]

        over_blocks(mask_block, jnp.zeros((1, tq), F32))

    nsub = tq // BIAS_TILE

    def bias_block(h, diag):
        same = tb_ref[h, 1]
        prev = tb_ref[h, 0]
        far = jnp.broadcast_to(tb_ref[h, 0, 0:1, 0:1], (BIAS_TILE, BIAS_TILE))
        if diag:
            grid = [[same, prev], [far, same]]
        else:
            grid = [[far, far], [prev, far]]
        return jnp.concatenate([jnp.concatenate(r[:nsub], axis=1) for r in grid], axis=0)

    def produce(h, kb, kind, buf, gated=False):
        t_buf, s_buf = buf
        g, j = divmod(h, GROUP)
        lanes = slice(j * tq, (j + 1) * tq)
        t = jnp.dot(k_ref[0, g, key_rows(kb), :], qt_ref[0, g, 0, :, lanes],
                    preferred_element_type=F32).astype(BF16) + mb_scr[kb]
        if kind == "far":
            shift = jnp.broadcast_to(tb_ref[h, 0, 0:1, 0:1], (1, tq))
        else:
            shift = jnp.zeros((1, tq), F32)
            gate = jnp.where(nkb >= 2, 0.0, NEG_INF) if gated else 0.0
            t = t + (bias_block(h, kind == "diag") + gate).astype(BF16)
        t_buf[g, :, lanes] = t
        s_buf[g, 0:1, lanes] = jnp.max(t, axis=0, keepdims=True).astype(F32) + shift
        s_buf[g, 1:2, lanes] = shift

    def consume(h, kb, buf):
        t_buf, s_buf = buf
        g, j = divmod(h, GROUP)
        lanes = slice(j * tq, (j + 1) * tq)
        m = m_scr[g, :, lanes]
        shift = s_buf[g, 1:2, lanes]
        m_new = jnp.maximum(m, s_buf[g, 0:1, lanes])
        sub = (jnp.where(m_new == NEG_INF, 0.0, m_new) - shift).astype(BF16)
        m_used = sub.astype(F32) + shift
        p = jnp.exp2(t_buf[g, :, lanes] - sub)
        acc_scr[g, :, lanes] = jnp.exp2(m - m_used) * acc_scr[g, :, lanes] + jnp.dot(
            vt_ref[0, g, kb], p, preferred_element_type=F32)
        m_scr[g, :, lanes] = jnp.where(m_new == NEG_INF, NEG_INF, m_used)

    def stage(new=None, old=None):
        for h in range(N_HEADS):
            if new is not None:
                produce(h, *new)
            if old is not None:
                consume(h, *old)

    m_scr[...] = jnp.full(m_scr.shape, NEG_INF, F32)
    acc_scr[...] = jnp.zeros(acc_scr.shape, F32)

    buf_a, buf_b = (ta_scr, sa_scr), (tb_scr, sb_scr)
    n_far = jnp.maximum(nkb - 2, 0)
    off = jnp.maximum(nkb - 2, 0)
    stage(new=(last, "diag", buf_a))
    stage(new=(off, "off", buf_b, True), old=(last, buf_a))

    def far_pair(it, carry):
        f = nkb - 3 - 2 * it
        stage(new=(f, "far", buf_a), old=(f + 1, buf_b))
        stage(new=(f - 1, "far", buf_b), old=(f, buf_a))
        return carry

    lax.fori_loop(0, n_far // 2, far_pair, 0)

    @pl.when(n_far % 2 == 1)
    def _():
        stage(new=(0, "far", buf_a), old=(1, buf_b))
        stage(old=(0, buf_a))

    @pl.when(n_far % 2 == 0)
    def _():
        stage(old=(jnp.maximum(nkb - 2 - n_far, 0), buf_b))

    heads_t = []
    for g in range(N_KV_HEADS):
        acc = acc_scr[g]
        o = acc[:HEAD_DIM] / acc[HEAD_DIM:HEAD_DIM + 1]
        heads_t += [o[:, j * tq:(j + 1) * tq] for j in range(GROUP)]
    o_ref[0] = jnp.concatenate(heads_t, axis=0).T.astype(o_ref.dtype)


def _attention(qt, qit, wi_t, k_hm, vt, kib, tb, tri, *, tq, past, ksel):
    b, _, nq, _, _ = qt.shape
    lk = k_hm.shape[2]
    nkb_max = lk // KEY_BLOCK
    blocks = (nkb_max, KEY_BLOCK, tq)
    pairs = ((nkb_max + 1) // 2, 2 * KEY_BLOCK, tq)
    return pl.pallas_call(
        functools.partial(_attn_kernel, tq=tq, past=past, ksel=ksel),
        grid=(b, nq),
        in_specs=[
            pl.BlockSpec((1, N_KV_HEADS, 1, HEAD_DIM, GROUP * tq), lambda bi, i: (bi, 0, i, 0, 0)),
            pl.BlockSpec((1, 1, IDX_DIM, IDX_HEADS * tq), lambda bi, i: (bi, i, 0, 0)),
            pl.BlockSpec((1, IDX_HEADS, tq), lambda bi, i: (bi, 0, i)),
            pl.BlockSpec((1, N_KV_HEADS, lk, HEAD_DIM), lambda bi, i: (bi, 0, 0, 0)),
            pl.BlockSpec((1, N_KV_HEADS, nkb_max, vt.shape[3], KEY_BLOCK),
                         lambda bi, i: (bi, 0, 0, 0, 0)),
            pl.BlockSpec((1, lk, IDX_DIM), lambda bi, i: (bi, 0, 0)),
            pl.BlockSpec((N_HEADS, 2, BIAS_TILE, BIAS_TILE), lambda bi, i: (0, 0, 0, 0)),
            pl.BlockSpec((KEY_BLOCK, KEY_BLOCK), lambda bi, i: (0, 0)),
        ],
        out_specs=pl.BlockSpec((1, tq, N_HEADS * HEAD_DIM), lambda bi, i: (bi, i, 0)),
        out_shape=jax.ShapeDtypeStruct((b, nq * tq, N_HEADS * HEAD_DIM), BF16),
        scratch_shapes=[pltpu.VMEM(blocks, I32), pltpu.VMEM(pairs, I16), pltpu.VMEM(pairs, I16),
                        pltpu.VMEM(blocks, BF16),
                        pltpu.VMEM((N_KV_HEADS, 1, GROUP * tq), F32),
                        pltpu.VMEM((N_KV_HEADS, vt.shape[3], GROUP * tq), F32)]
        + 2 * [pltpu.VMEM((N_KV_HEADS, KEY_BLOCK, GROUP * tq), BF16),
               pltpu.VMEM((N_KV_HEADS, 2, GROUP * tq), F32)],
        compiler_params=_params("arbitrary", "arbitrary"),
        name="attention",
    )(qt, qit, wi_t, k_hm, vt, kib, tb, tri)


def _post_kernel(x_ref, mod_ref, cv_ref, sgc_ref, o_ref, sga_ref, gm_ref,
                 lg_ref, lb_ref, wc_ref, wa_ref, wo_ref, gp_ref, y_ref, *, d_model):
    cv = cv_ref[0]
    mu = jnp.mean(cv, axis=-1, keepdims=True)
    dv = cv - mu
    var = jnp.mean(dv * dv, axis=-1, keepdims=True)
    yc = _silu(dv * lax.rsqrt(var + EPS) * lg_ref[...] + lb_ref[...])
    yc = (yc * sgc_ref[0].astype(F32)).astype(BF16)
    y_c = jnp.dot(yc, wc_ref[...], preferred_element_type=F32)
    ya = (o_ref[0].astype(F32) * sga_ref[0].astype(F32)).astype(BF16)
    y_a = jnp.dot(ya, wa_ref[...], preferred_element_type=F32)
    gm = gm_ref[0].astype(F32)
    mix = (gm[:, :d_model] * y_c + gm[:, d_model:] * y_a).astype(BF16)
    y = jnp.dot(mix, wo_ref[...], preferred_element_type=F32)
    y = y * lax.rsqrt(jnp.mean(y * y, axis=-1, keepdims=True) + EPS) * gp_ref[...]
    y_ref[0] = x_ref[0] + mod_ref[0, 2:3, :] * y


def _post(x, mod3, cv, sgc, o, sga, gm, ln_g, ln_b, wc, wa, wo, g_post):
    b, l, d = x.shape
    d_conv = cv.shape[2]
    d_attn = o.shape[2]
    tt = min(256, l)

    def row(width):
        return pl.BlockSpec((1, tt, width), lambda bi, i: (bi, i, 0))

    def const(shape):
        return pl.BlockSpec(shape, lambda bi, i: (0,) * len(shape))

    return pl.pallas_call(
        functools.partial(_post_kernel, d_model=d),
        grid=(b, l // tt),
        in_specs=[
            row(d),
            pl.BlockSpec((1, 3, d), lambda bi, i: (bi, 0, 0)),
            row(d_conv), row(d_conv), row(d_attn), row(d_attn), row(2 * d),
            const((1, d_conv)), const((1, d_conv)),
            const((d_conv, d)), const((d_attn, d)), const((d, d)), const((1, d)),
        ],
        out_specs=row(d),
        out_shape=jax.ShapeDtypeStruct((b, l, d), F32),
        compiler_params=_params("arbitrary", "arbitrary"),
        name="post",
    )(x, mod3, cv, sgc, o, sga, gm, ln_g, ln_b, wc, wa, wo, g_post)


def _pad_axis(t, axis, size):
    if t.shape[axis] == size:
        return t
    widths = [(0, 0)] * t.ndim
    widths[axis] = (0, size - t.shape[axis])
    return jnp.pad(t, widths)


def _sublayer(x, l_true, mod3, conv_prev, past, tb, tri, w):
    b, l, d = x.shape
    d_conv = w["conv_w"].shape[1]
    d_attn = N_HEADS * HEAD_DIM
    prev = jnp.pad(conv_prev, ((0, 0), (CONV_HALO - (CONV_W - 1), 0), (0, 0)))
    (u, conv, sgc, qt, k, v, k_hm, vt, sga, qit, ki, kib, wi_t, gm) = _inproj(
        x, mod3, w["g_pre"], w["w_in"], prev, w["conv_w"], w["conv_b"], d_conv=d_conv, d_attn=d_attn)

    if past is None:
        p_len = 0
    else:
        ck, cv, cki = past
        p_len = ck.shape[1]
        k_hm = jnp.concatenate([ck.transpose(0, 2, 1, 3).astype(BF16), k_hm], axis=2)
        ones_rows = jnp.zeros((b, N_KV_HEADS, PACKED_ROWS, p_len), BF16).at[:, :, 0].set(1.0)
        cvt = jnp.concatenate([cv.transpose(0, 2, 3, 1).astype(BF16), ones_rows], axis=2)
        cvt = cvt.reshape(b, N_KV_HEADS, cvt.shape[2], p_len // KEY_BLOCK, KEY_BLOCK)
        vt = jnp.concatenate([cvt.transpose(0, 1, 3, 2, 4), _pad_axis(vt, 4, KEY_BLOCK)], axis=2)
        kib = jnp.concatenate([cki.astype(BF16), kib], axis=1)
    ksel = min(TOPK_MAX, (p_len + l_true) // 4)
    tq = min(KEY_BLOCK, l)
    lk_pad = -(-(p_len + l) // KEY_BLOCK) * KEY_BLOCK
    assert p_len % KEY_BLOCK == 0 and (tq == KEY_BLOCK or l == tq)
    assert (p_len + l_true) % CHUNK == 0 and ksel <= KEY_BLOCK
    o = _attention(qt, qit, wi_t, _pad_axis(k_hm, 2, lk_pad), vt, _pad_axis(kib, 1, lk_pad),
                   tb, tri, tq=tq, past=p_len, ksel=ksel)

    y = _post(x, mod3, conv, sgc, o, sga, gm, w["ln_g"], w["ln_b"],
              w["w_conv_out"], w["w_attn_out"], w["w_out"], w["g_post"])
    conv_new = jnp.concatenate([conv_prev, u[:, :l_true]], axis=1)[:, -(CONV_W - 1):]
    return (y, conv_new, k[:, :l_true].reshape(b, l_true, N_KV_HEADS, HEAD_DIM),
            v[:, :l_true].reshape(b, l_true, N_KV_HEADS, HEAD_DIM), ki[:, :l_true])


def _pack_w_in(w_in, d_conv, d_attn, d_model):
    kvd = N_KV_HEADS * HEAD_DIM
    splits = (2 * d_conv, d_conv, d_attn, kvd, kvd, d_attn, IDX_HEADS * IDX_DIM, IDX_DIM,
              IDX_HEADS, 2 * d_model)
    offs = [0]
    for s in splits:
        offs.append(offs[-1] + s)
    seg = [w_in[:, offs[j]:offs[j + 1]] for j in range(len(splits))]
    kw = jnp.concatenate([seg[7], seg[8]], axis=1)
    kw = jnp.pad(kw, ((0, 0), (0, LANES - kw.shape[1])))
    return jnp.concatenate(seg[:7] + [kw, seg[9]], axis=1).astype(BF16)


def kernel(x_prompt, x_sample, c_prompt, c_sample, cache_k, cache_v, cache_kidx, state_conv,
           rel_bias, ada_w, ada_b, norm_pre, norm_post, w_in, conv_w, conv_b, conv_ln_g,
           conv_ln_b, w_conv_out, w_attn_out, w_out):
    depth = ada_w.shape[0]
    bp, _, d = x_prompt.shape
    d_conv = conv_w.shape[2]
    d_attn = N_HEADS * HEAD_DIM
    mod = _ada(jnp.concatenate([c_prompt, c_sample], axis=0), ada_w, ada_b)
    mod = mod.reshape(depth, mod.shape[1], 3, d)
    tb = _bias_tiles(rel_bias)
    tri = (jnp.arange(KEY_BLOCK)[:, None] >= jnp.arange(KEY_BLOCK)[None, :]).astype(BF16)

    lp, ls = x_prompt.shape[1], x_sample.shape[1]
    xp, xs = x_prompt, _pad_axis(x_sample, 1, -(-ls // LANES) * LANES)
    outs = [[] for _ in range(8)]
    for l in range(depth):
        w = dict(
            g_pre=norm_pre[l][None], g_post=norm_post[l][None],
            w_in=_pack_w_in(w_in[l], d_conv, d_attn, d),
            conv_w=conv_w[l], conv_b=conv_b[l][None], ln_g=conv_ln_g[l][None],
            ln_b=conv_ln_b[l][None], w_conv_out=w_conv_out[l].astype(BF16),
            w_attn_out=w_attn_out[l].astype(BF16), w_out=w_out[l].astype(BF16))
        zeros_conv = jnp.zeros((bp, CONV_W - 1, d_conv), xp.dtype)
        xp, cp, kp, vp, kip = _sublayer(xp, lp, mod[l, :bp], zeros_conv, None, tb, tri, w)
        xs, cs, kss, vss, kis = _sublayer(xs, ls, mod[l, bp:], state_conv[l],
                                          (cache_k[l], cache_v[l], cache_kidx[l]), tb, tri, w)
        for lst, val in zip(outs, (kp, vp, kip, cp, kss, vss, kis, cs)):
            lst.append(val)
    return (xp, xs[:, :ls]) + tuple(jnp.stack(o) for o in outs)
```

```python
import functools
import math

import jax
import jax.numpy as jnp
from jax import lax
from jax.experimental import pallas as pl
from jax.experimental.pallas import tpu as pltpu

F32 = jnp.float32
BF16 = jnp.bfloat16
I32 = jnp.int32
I16 = jnp.int16

CHUNK = 64
CONV_W = 31
N_HEADS = 16
N_KV_HEADS = 4
HEAD_DIM = 64
GROUP = N_HEADS // N_KV_HEADS
IDX_HEADS = 8
IDX_DIM = 64
TOPK_MAX = 256
N_BUCKETS = 32
MAX_DIST = 128
EPS = 1e-6

LANES = 128
SUBLANES = 8
PACKED_ROWS = 16
KEY_BLOCK = 256
BIAS_TILE = 128
CONV_HALO = 32
VMEM_LIMIT_BYTES = 56 * 1024 * 1024
NEG_INF = float("-inf")
INT_MIN = -2 ** 31
HALF_MIN = -2 ** 15
LOG2E = 1.4426950408889634


def _sigmoid(x):
    return 1.0 / (1.0 + jnp.exp(-x))


def _silu(x):
    return x * _sigmoid(x)


def _params(*sem):
    return pltpu.CompilerParams(dimension_semantics=sem, vmem_limit_bytes=VMEM_LIMIT_BYTES)


def _ada_kernel(c_ref, w_ref, b_ref, o_ref):
    c = c_ref[...]
    o_ref[0] = jnp.dot(_silu(c), w_ref[0], preferred_element_type=F32) + b_ref[0]


def _ada(c_all, ada_w, ada_b):
    depth, d, _ = ada_w.shape
    bc = c_all.shape[0]
    return pl.pallas_call(
        _ada_kernel,
        grid=(depth, 3),
        in_specs=[
            pl.BlockSpec((bc, d), lambda l, n: (0, 0)),
            pl.BlockSpec((1, d, d), lambda l, n: (l, 0, n)),
            pl.BlockSpec((1, 1, d), lambda l, n: (l, 0, n)),
        ],
        out_specs=pl.BlockSpec((1, bc, d), lambda l, n: (l, 0, n)),
        out_shape=jax.ShapeDtypeStruct((depth, bc, 3 * d), F32),
        compiler_params=_params("arbitrary", "arbitrary"),
        name="ada",
    )(c_all, ada_w, ada_b.reshape(depth, 1, 3 * d))


def _bias_kernel(rb_ref, o_ref):
    r = lax.broadcasted_iota(I32, (BIAS_TILE, BIAS_TILE), 0)
    c = lax.broadcasted_iota(I32, (BIAS_TILE, BIAS_TILE), 1)
    nb = N_BUCKETS // 2
    max_exact = nb // 2
    for t in range(2):
        rel = (t - 1) * BIAS_TILE + r - c
        ret = jnp.where(rel > 0, nb, 0)
        n = jnp.abs(rel)
        nf = jnp.maximum(n, 1).astype(F32)
        large = max_exact + (jnp.log(nf / max_exact) / math.log(MAX_DIST / max_exact)
                             * (nb - max_exact)).astype(I32)
        large = jnp.minimum(large, nb - 1)
        bucket = ret + jnp.where(n < max_exact, n, large)
        for h in range(N_HEADS):
            acc = jnp.zeros((BIAS_TILE, BIAS_TILE), F32)
            for b in range(N_BUCKETS):
                acc = jnp.where(bucket == b, rb_ref[b, h], acc)
            o_ref[h, t] = acc * LOG2E


def _bias_tiles(rel_bias):
    return pl.pallas_call(
        _bias_kernel,
        in_specs=[pl.BlockSpec(memory_space=pltpu.SMEM)],
        out_specs=pl.BlockSpec(memory_space=pltpu.VMEM),
        out_shape=jax.ShapeDtypeStruct((N_HEADS, 2, BIAS_TILE, BIAS_TILE), F32),
        name="bias_tiles",
    )(rel_bias)


def _seg_offsets(d_conv, d_attn):
    widths = (2 * d_conv, d_conv, d_attn, N_KV_HEADS * HEAD_DIM, N_KV_HEADS * HEAD_DIM, d_attn,
              IDX_HEADS * IDX_DIM, LANES, None)
    offs, acc = [], 0
    for w in widths:
        offs.append(acc)
        if w is not None:
            acc += w
    return offs


def _inproj_kernel(x_ref, mod_ref, g_ref, w_ref,
                   u_ref, sgc_ref, qt_ref, k_ref, v_ref, kb_ref, vt_ref, sga_ref, qit_ref,
                   ki_ref, kib_ref, wi_ref, gm_ref, *, tm, d_conv, d_attn, d_model):
    x = x_ref[0]
    ms = jnp.mean(x * x, axis=-1, keepdims=True)
    y = x * lax.rsqrt(ms + EPS) * g_ref[...]
    shift = mod_ref[0, 0:1, :]
    scale = mod_ref[0, 1:2, :]
    h = (y * (1.0 + scale) + shift).astype(BF16)
    o_a, o_gc, o_q, o_k, o_v, o_ga, o_qi, o_kw, o_gm = _seg_offsets(d_conv, d_attn)

    def mm(lo, width):
        return jnp.dot(h, w_ref[:, lo:lo + width], preferred_element_type=F32)

    a = mm(o_a, 2 * d_conv)
    u_ref[0] = a[:, :d_conv] * _sigmoid(a[:, d_conv:])
    sgc_ref[0] = _silu(mm(o_gc, d_conv)).astype(BF16)
    zqt = (mm(o_q, d_attn) * (HEAD_DIM ** -0.5 * LOG2E)).T
    for hh in range(N_HEADS):
        g, j = divmod(hh, GROUP)
        qt_ref[0, g, 0, :, j * tm:(j + 1) * tm] = (
            zqt[hh * HEAD_DIM:(hh + 1) * HEAD_DIM].astype(BF16))
    zk = mm(o_k, N_KV_HEADS * HEAD_DIM)
    zv = mm(o_v, N_KV_HEADS * HEAD_DIM)
    k_ref[0] = zk
    v_ref[0] = zv
    zvt = zv.T
    ones_rows = jnp.where(lax.broadcasted_iota(I32, (PACKED_ROWS, tm), 0) == 0, 1.0, 0.0)
    for g in range(N_KV_HEADS):
        kb_ref[0, g] = zk[:, g * HEAD_DIM:(g + 1) * HEAD_DIM].astype(BF16)
        vt_ref[0, g, 0, 0:HEAD_DIM] = zvt[g * HEAD_DIM:(g + 1) * HEAD_DIM].astype(BF16)
        vt_ref[0, g, 0, HEAD_DIM:] = ones_rows.astype(BF16)
    sga_ref[0] = _silu(mm(o_ga, d_attn)).astype(BF16)
    zqit = (mm(o_qi, IDX_HEADS * IDX_DIM) * (IDX_DIM ** -0.5)).T
    for hh in range(IDX_HEADS):
        qit_ref[0, 0, :, hh * tm:(hh + 1) * tm] = zqit[hh * IDX_DIM:(hh + 1) * IDX_DIM].astype(BF16)
    zkw = mm(o_kw, LANES)
    zki = zkw[:, :IDX_DIM]
    ki_ref[0] = zki
    kib_ref[0] = zki.astype(BF16)
    wi_ref[0] = zkw.T[IDX_DIM:IDX_DIM + IDX_HEADS] * (IDX_HEADS ** -0.5)
    gm_ref[0] = _sigmoid(mm(o_gm, 2 * d_model)).astype(BF16)


def _inproj(x, mod3, g_pre, w_pad, *, d_conv, d_attn):
    b, l, d = x.shape
    tm = min(256, l)
    nw = w_pad.shape[1]
    kvd = N_KV_HEADS * HEAD_DIM

    def row(width):
        return pl.BlockSpec((1, tm, width), lambda bi, i: (bi, i, 0))

    def heads(n, width):
        return pl.BlockSpec((1, n, tm, width), lambda bi, i: (bi, 0, i, 0))

    nq = l // tm
    vrows = HEAD_DIM + PACKED_ROWS
    out_shape = (
        jax.ShapeDtypeStruct((b, l, d_conv), F32),
        jax.ShapeDtypeStruct((b, l, d_conv), BF16),
        jax.ShapeDtypeStruct((b, N_KV_HEADS, nq, HEAD_DIM, GROUP * tm), BF16),
        jax.ShapeDtypeStruct((b, l, kvd), F32),
        jax.ShapeDtypeStruct((b, l, kvd), F32),
        jax.ShapeDtypeStruct((b, N_KV_HEADS, l, HEAD_DIM), BF16),
        jax.ShapeDtypeStruct((b, N_KV_HEADS, nq, vrows, tm), BF16),
        jax.ShapeDtypeStruct((b, l, d_attn), BF16),
        jax.ShapeDtypeStruct((b, nq, IDX_DIM, IDX_HEADS * tm), BF16),
        jax.ShapeDtypeStruct((b, l, IDX_DIM), F32),
        jax.ShapeDtypeStruct((b, l, IDX_DIM), BF16),
        jax.ShapeDtypeStruct((b, IDX_HEADS, l), F32),
        jax.ShapeDtypeStruct((b, l, 2 * d), BF16),
    )
    out_specs = (row(d_conv), row(d_conv),
                 pl.BlockSpec((1, N_KV_HEADS, 1, HEAD_DIM, GROUP * tm), lambda bi, i: (bi, 0, i, 0, 0)),
                 row(kvd), row(kvd), heads(N_KV_HEADS, HEAD_DIM),
                 pl.BlockSpec((1, N_KV_HEADS, 1, vrows, tm), lambda bi, i: (bi, 0, i, 0, 0)),
                 row(d_attn),
                 pl.BlockSpec((1, 1, IDX_DIM, IDX_HEADS * tm), lambda bi, i: (bi, i, 0, 0)),
                 row(IDX_DIM), row(IDX_DIM),
                 pl.BlockSpec((1, IDX_HEADS, tm), lambda bi, i: (bi, 0, i)),
                 row(2 * d))
    return pl.pallas_call(
        functools.partial(_inproj_kernel, tm=tm, d_conv=d_conv, d_attn=d_attn, d_model=d),
        grid=(b, l // tm),
        in_specs=[
            pl.BlockSpec((1, tm, d), lambda bi, i: (bi, i, 0)),
            pl.BlockSpec((1, 3, d), lambda bi, i: (bi, 0, 0)),
            pl.BlockSpec((1, d), lambda bi, i: (0, 0)),
            pl.BlockSpec((d, nw), lambda bi, i: (0, 0), pipeline_mode=pl.Buffered(1)),
        ],
        out_specs=out_specs,
        out_shape=out_shape,
        compiler_params=_params("arbitrary", "arbitrary"),
        name="inproj",
    )(x, mod3, g_pre, w_pad)


def _attn_kernel(qt_ref, qit_ref, wi_ref, k_ref, vt_ref, ki_ref, tb_ref, tri_ref, o_ref,
                 key_scr, hi_scr, lo_scr, mb_scr, m_scr, acc_scr, ta_scr, sa_scr, tb_scr, sb_scr,
                 bias_scr, *, tq, past, ksel):
    i = pl.program_id(1)
    q0 = past + i * tq
    nkb = (q0 + tq + KEY_BLOCK - 1) // KEY_BLOCK

    def key_rows(kb):
        return pl.ds(pl.multiple_of(kb * KEY_BLOCK, KEY_BLOCK), KEY_BLOCK)

    qit = qit_ref[0, 0]
    wi = wi_ref[0]
    last = nkb - 1
    last_ok = (lax.broadcasted_iota(I32, (KEY_BLOCK, tq), 0) // CHUNK
               <= lax.broadcasted_iota(I32, (KEY_BLOCK, tq), 1) // CHUNK)

    def score_block(kb, is_last):
        ki_blk = ki_ref[0, key_rows(kb), :]
        acc = jnp.zeros((KEY_BLOCK, tq), F32)
        for h in range(IDX_HEADS):
            s = jnp.dot(ki_blk, qit[:, h * tq:(h + 1) * tq], preferred_element_type=F32)
            acc = acc + jnp.maximum(s, 0.0) * wi[h:h + 1, :]
        sc = jnp.where(last_ok, acc, NEG_INF) if is_last else acc
        bits = pltpu.bitcast(sc, I32)
        key = bits ^ ((bits >> 31) & 0x7FFFFFFF)
        key_scr[kb] = key
        half = pl.ds(pl.multiple_of((kb % 2) * KEY_BLOCK, KEY_BLOCK), KEY_BLOCK)
        hi_scr[kb // 2, half, :] = (key >> 16).astype(I16)
        lo_scr[kb // 2, half, :] = ((key & 0xFFFF) + HALF_MIN).astype(I16)

    def over_blocks(fn, init=0):
        carry = lax.fori_loop(0, last, lambda kb, c: fn(kb, False, c), init)
        return fn(last, True, carry)

    def score_pair(kp, carry):
        score_block(2 * kp, False)
        score_block(2 * kp + 1, False)
        return carry

    lax.fori_loop(0, last // 2, score_pair, 0)

    @pl.when(last % 2 == 1)
    def _():
        score_block(last - 1, False)

    score_block(last, True)

    npair = (nkb + 1) // 2

    @pl.when(nkb % 2 == 1)
    def _():
        floor16 = jnp.full((KEY_BLOCK, tq), HALF_MIN, I16)
        hi_scr[npair - 1, KEY_BLOCK:, :] = floor16
        lo_scr[npair - 1, KEY_BLOCK:, :] = floor16

    def count16(scr, pred, visit=None):
        groups = 2 * KEY_BLOCK // PACKED_ROWS

        def body(kp, acc):
            x = scr[kp]
            if visit is not None:
                visit(kp, x)
            hit = jnp.where(pred(x), jnp.int16(1), jnp.int16(0))
            hit = hit.reshape(groups, PACKED_ROWS, tq)
            parts = [hit[r] for r in range(groups)]
            while len(parts) > 1:
                parts = [parts[r] + parts[r + 1] for r in range(0, len(parts), 2)]
            return acc + parts[0]
        acc = lax.fori_loop(0, npair, body, jnp.zeros((PACKED_ROWS, tq), I16))
        return acc.astype(I32).sum(axis=0, keepdims=True)

    def search16(scr, target):
        def refine(it, state):
            thr, n_at = state
            cand = thr + jnp.left_shift(jnp.int32(1), 15 - it)
            c16 = cand.astype(I16)
            n = count16(scr, lambda x: x >= c16)
            return jnp.where(n >= target, cand, thr), jnp.where(n >= target, n, n_at)
        every = jnp.full((1, tq), 2 * KEY_BLOCK, I32) * npair
        return lax.fori_loop(0, 16, refine, (jnp.full((1, tq), HALF_MIN, I32), every))

    thr_hi, _ = search16(hi_scr, ksel)
    thr_hi16 = thr_hi.astype(I16)

    def restrict(kp, hi):
        lo_scr[kp] = jnp.where(hi == thr_hi16, lo_scr[kp], jnp.int16(HALF_MIN))

    above = count16(hi_scr, lambda x: x > thr_hi16, visit=restrict)
    thr_lo, n_lo = search16(lo_scr, ksel - above)
    thr = thr_hi * 65536 + (thr_lo - HALF_MIN)
    surplus = jnp.max(above + n_lo) > ksel

    @pl.when(jnp.logical_not(surplus))
    def _():
        def mask_block(kb, is_last, carry):
            keep = jnp.where(key_scr[kb] >= thr, 0.0, NEG_INF)
            mb_scr[kb] = (jnp.where(last_ok, keep, NEG_INF) if is_last else keep).astype(BF16)
            return carry

        over_blocks(mask_block)

    @pl.when(surplus)
    def _():
        def count_gt(kb, acc):
            hit = (key_scr[kb] > thr).astype(I32)
            return acc + hit.reshape(KEY_BLOCK // 8, 8, tq).sum(axis=0)

        n_gt = lax.fori_loop(0, nkb, count_gt, jnp.zeros((8, tq), I32)).sum(axis=0, keepdims=True)
        need = (ksel - n_gt).astype(F32)

        def mask_block(kb, is_last, ties_before):
            kk = key_scr[kb]
            eq = kk == thr
            rank = jnp.dot(tri_ref[...], jnp.where(eq, 1.0, 0.0).astype(BF16),
                           preferred_element_type=F32) + ties_before
            tie = jnp.where(eq, jnp.where(rank <= need, 0.0, NEG_INF), NEG_INF)
            keep = jnp.where(kk > thr, 0.0, tie)
            mb_scr[kb] = (jnp.where(last_ok, keep, NEG_INF) if is_last else keep).astype(BF16)
            return rank[KEY_BLOCK - 1:KEY_BLOCK, :]

        over_blocks(mask_block, jnp.zeros((1, tq), F32))

    nsub = tq // BIAS_TILE

    def bias_block(h, diag):
        same = tb_ref[h, 1]
        prev = tb_ref[h, 0]
        far = jnp.broadcast_to(tb_ref[h, 0, 0:1, 0:1], (BIAS_TILE, BIAS_TILE))
        if diag:
            grid = [[same, prev], [far, same]]
        else:
            grid = [[far, far], [prev, far]]
        return jnp.concatenate([jnp.concatenate(r[:nsub], axis=1) for r in grid], axis=0)

    @pl.when(jnp.logical_and(pl.program_id(0) == 0, i == 0))
    def _():
        for h in range(N_HEADS):
            bias_scr[0, h] = bias_block(h, False).astype(BF16)
            bias_scr[1, h] = bias_block(h, True).astype(BF16)

    def produce(h, kb, kind, buf, gated=False):
        t_buf, s_buf = buf
        g, j = divmod(h, GROUP)
        lanes = slice(j * tq, (j + 1) * tq)
        t = jnp.dot(k_ref[0, g, key_rows(kb), :], qt_ref[0, g, 0, :, lanes],
                    preferred_element_type=F32).astype(BF16) + mb_scr[kb]
        if kind == "far":
            shift = jnp.broadcast_to(tb_ref[h, 0, 0:1, 0:1], (1, tq))
        else:
            shift = jnp.zeros((1, tq), F32)
            t = t + bias_scr[int(kind == "diag"), h]
            if gated:
                t = t + jnp.where(nkb >= 2, 0.0, NEG_INF).astype(BF16)
        t_buf[g, :, lanes] = t
        s_buf[g, 0:1, lanes] = jnp.max(t, axis=0, keepdims=True).astype(F32) + shift
        s_buf[g, 1:2, lanes] = shift

    def consume(h, kb, buf):
        t_buf, s_buf = buf
        g, j = divmod(h, GROUP)
        lanes = slice(j * tq, (j + 1) * tq)
        m = m_scr[g, :, lanes]
        shift = s_buf[g, 1:2, lanes]
        m_new = jnp.maximum(m, s_buf[g, 0:1, lanes])
        sub = (jnp.where(m_new == NEG_INF, 0.0, m_new) - shift).astype(BF16)
        m_used = sub.astype(F32) + shift
        p = jnp.exp2(t_buf[g, :, lanes] - sub)
        acc_scr[g, :, lanes] = jnp.exp2(m - m_used) * acc_scr[g, :, lanes] + jnp.dot(
            vt_ref[0, g, kb], p, preferred_element_type=F32)
        m_scr[g, :, lanes] = jnp.where(m_new == NEG_INF, NEG_INF, m_used)

    def stage(new=None, old=None):
        for h in range(N_HEADS):
            if new is not None:
                produce(h, *new)
            if old is not None:
                consume(h, *old)

    m_scr[...] = jnp.full(m_scr.shape, NEG_INF, F32)
    acc_scr[...] = jnp.zeros(acc_scr.shape, F32)

    buf_a, buf_b = (ta_scr, sa_scr), (tb_scr, sb_scr)
    n_far = jnp.maximum(nkb - 2, 0)
    off = jnp.maximum(nkb - 2, 0)
    stage(new=(last, "diag", buf_a))
    stage(new=(off, "off", buf_b, True), old=(last, buf_a))

    def far_pair(it, carry):
        f = nkb - 3 - 2 * it
        stage(new=(f, "far", buf_a), old=(f + 1, buf_b))
        stage(new=(f - 1, "far", buf_b), old=(f, buf_a))
        return carry

    lax.fori_loop(0, n_far // 2, far_pair, 0)

    @pl.when(n_far % 2 == 1)
    def _():
        stage(new=(0, "far", buf_a), old=(1, buf_b))
        stage(old=(0, buf_a))

    @pl.when(n_far % 2 == 0)
    def _():
        stage(old=(jnp.maximum(nkb - 2 - n_far, 0), buf_b))

    heads_t = []
    for g in range(N_KV_HEADS):
        acc = acc_scr[g]
        o = acc[:HEAD_DIM] / acc[HEAD_DIM:HEAD_DIM + 1]
        heads_t += [o[:, j * tq:(j + 1) * tq] for j in range(GROUP)]
    o_ref[0] = jnp.concatenate(heads_t, axis=0).T.astype(o_ref.dtype)


def _attention(qt, qit, wi_t, k_hm, vt, kib, tb, tri, *, tq, past, ksel):
    b, _, nq, _, _ = qt.shape
    lk = k_hm.shape[2]
    nkb_max = lk // KEY_BLOCK
    blocks = (nkb_max, KEY_BLOCK, tq)
    pairs = ((nkb_max + 1) // 2, 2 * KEY_BLOCK, tq)
    return pl.pallas_call(
        functools.partial(_attn_kernel, tq=tq, past=past, ksel=ksel),
        grid=(b, nq),
        in_specs=[
            pl.BlockSpec((1, N_KV_HEADS, 1, HEAD_DIM, GROUP * tq), lambda bi, i: (bi, 0, i, 0, 0)),
            pl.BlockSpec((1, 1, IDX_DIM, IDX_HEADS * tq), lambda bi, i: (bi, i, 0, 0)),
            pl.BlockSpec((1, IDX_HEADS, tq), lambda bi, i: (bi, 0, i)),
            pl.BlockSpec((1, N_KV_HEADS, lk, HEAD_DIM), lambda bi, i: (bi, 0, 0, 0)),
            pl.BlockSpec((1, N_KV_HEADS, nkb_max, vt.shape[3], KEY_BLOCK),
                         lambda bi, i: (bi, 0, 0, 0, 0)),
            pl.BlockSpec((1, lk, IDX_DIM), lambda bi, i: (bi, 0, 0)),
            pl.BlockSpec((N_HEADS, 2, BIAS_TILE, BIAS_TILE), lambda bi, i: (0, 0, 0, 0)),
            pl.BlockSpec((KEY_BLOCK, KEY_BLOCK), lambda bi, i: (0, 0)),
        ],
        out_specs=pl.BlockSpec((1, tq, N_HEADS * HEAD_DIM), lambda bi, i: (bi, i, 0)),
        out_shape=jax.ShapeDtypeStruct((b, nq * tq, N_HEADS * HEAD_DIM), BF16),
        scratch_shapes=[pltpu.VMEM(blocks, I32), pltpu.VMEM(pairs, I16), pltpu.VMEM(pairs, I16),
                        pltpu.VMEM(blocks, BF16),
                        pltpu.VMEM((N_KV_HEADS, 1, GROUP * tq), F32),
                        pltpu.VMEM((N_KV_HEADS, vt.shape[3], GROUP * tq), F32)]
        + 2 * [pltpu.VMEM((N_KV_HEADS, KEY_BLOCK, GROUP * tq), BF16),
               pltpu.VMEM((N_KV_HEADS, 2, GROUP * tq), F32)]
        + [pltpu.VMEM((2, N_HEADS, KEY_BLOCK, tq), BF16)],
        compiler_params=_params("arbitrary", "arbitrary"),
        name="attention",
    )(qt, qit, wi_t, k_hm, vt, kib, tb, tri)


def _post_kernel(x_ref, mod_ref, u_ref, halo_ref, prev_ref, sgc_ref, o_ref, sga_ref, gm_ref,
                 cw_ref, cb_ref, lg_ref, lb_ref, wc_ref, wa_ref, wo_ref, gp_ref,
                 y_ref, up_scr, sh_scr, cv_scr, *, tt, d_conv, d_model):
    i = pl.program_id(1)
    up_scr[0:CONV_HALO] = jnp.where(i == 0, prev_ref[0], halo_ref[0])
    up_scr[CONV_HALO:] = u_ref[0]
    span = tt + CONV_HALO - SUBLANES
    for s in range(1, SUBLANES):
        sh_scr[s - 1] = up_scr[pl.ds(s, span), :]
    pad = CONV_HALO - (CONV_W - 1)
    rt = min(tt, 128)
    for r0 in range(0, tt, rt):
        for c0 in range(0, d_conv, LANES):
            acc = jnp.broadcast_to(cb_ref[:, c0:c0 + LANES], (rt, LANES))
            for j in range(CONV_W):
                a, s = divmod(pad + j, SUBLANES)
                rows = pl.ds(r0 + SUBLANES * a, rt)
                win = up_scr[rows, c0:c0 + LANES] if s == 0 else sh_scr[s - 1, rows, c0:c0 + LANES]
                acc = acc + cw_ref[j:j + 1, c0:c0 + LANES] * win
            cv_scr[r0:r0 + rt, c0:c0 + LANES] = acc
    cv = cv_scr[...]
    mu = jnp.mean(cv, axis=-1, keepdims=True)
    dv = cv - mu
    var = jnp.mean(dv * dv, axis=-1, keepdims=True)
    yc = _silu(dv * lax.rsqrt(var + EPS) * lg_ref[...] + lb_ref[...])
    yc = (yc * sgc_ref[0].astype(F32)).astype(BF16)
    y_c = jnp.dot(yc, wc_ref[...], preferred_element_type=F32)
    ya = (o_ref[0].astype(F32) * sga_ref[0].astype(F32)).astype(BF16)
    y_a = jnp.dot(ya, wa_ref[...], preferred_element_type=F32)
    gm = gm_ref[0].astype(F32)
    mix = (gm[:, :d_model] * y_c + gm[:, d_model:] * y_a).astype(BF16)
    y = jnp.dot(mix, wo_ref[...], preferred_element_type=F32)
    y = y * lax.rsqrt(jnp.mean(y * y, axis=-1, keepdims=True) + EPS) * gp_ref[...]
    y_ref[0] = x_ref[0] + mod_ref[0, 2:3, :] * y


def _post(x, mod3, u, prev, sgc, o, sga, gm, conv_w, conv_b, ln_g, ln_b, wc, wa, wo, g_post):
    b, l, d = x.shape
    d_conv = u.shape[2]
    d_attn = o.shape[2]
    tt = min(256, l)
    hb = tt // CONV_HALO

    def row(width):
        return pl.BlockSpec((1, tt, width), lambda bi, i: (bi, i, 0))

    def const(shape):
        return pl.BlockSpec(shape, lambda bi, i: (0,) * len(shape))

    return pl.pallas_call(
        functools.partial(_post_kernel, tt=tt, d_conv=d_conv, d_model=d),
        grid=(b, l // tt),
        in_specs=[
            row(d),
            pl.BlockSpec((1, 3, d), lambda bi, i: (bi, 0, 0)),
            row(d_conv),
            pl.BlockSpec((1, CONV_HALO, d_conv), lambda bi, i: (bi, jnp.maximum(i * hb - 1, 0), 0)),
            pl.BlockSpec((1, CONV_HALO, d_conv), lambda bi, i: (bi, 0, 0)),
            row(d_conv), row(d_attn), row(d_attn), row(2 * d),
            const((CONV_W, d_conv)), const((1, d_conv)), const((1, d_conv)), const((1, d_conv)),
            const((d_conv, d)), const((d_attn, d)), const((d, d)), const((1, d)),
        ],
        out_specs=row(d),
        out_shape=jax.ShapeDtypeStruct((b, l, d), F32),
        scratch_shapes=[pltpu.VMEM((tt + CONV_HALO, d_conv), F32),
                        pltpu.VMEM((SUBLANES - 1, tt + CONV_HALO - SUBLANES, d_conv), F32),
                        pltpu.VMEM((tt, d_conv), F32)],
        compiler_params=_params("arbitrary", "arbitrary"),
        name="post",
    )(x, mod3, u, u, prev, sgc, o, sga, gm, conv_w, conv_b, ln_g, ln_b, wc, wa, wo, g_post)


def _pad_axis(t, axis, size):
    if t.shape[axis] == size:
        return t
    widths = [(0, 0)] * t.ndim
    widths[axis] = (0, size - t.shape[axis])
    return jnp.pad(t, widths)


def _sublayer(x, l_true, mod3, conv_prev, past, tb, tri, w):
    b, l, d = x.shape
    d_conv = w["conv_w"].shape[1]
    d_attn = N_HEADS * HEAD_DIM
    (u, sgc, qt, k, v, k_hm, vt, sga, qit, ki, kib, wi_t, gm) = _inproj(
        x, mod3, w["g_pre"], w["w_in"], d_conv=d_conv, d_attn=d_attn)

    if past is None:
        p_len = 0
    else:
        ck, cv, cki = past
        p_len = ck.shape[1]
        k_hm = jnp.concatenate([ck.transpose(0, 2, 1, 3).astype(BF16), k_hm], axis=2)
        ones_rows = jnp.zeros((b, N_KV_HEADS, PACKED_ROWS, p_len), BF16).at[:, :, 0].set(1.0)
        cvt = jnp.concatenate([cv.transpose(0, 2, 3, 1).astype(BF16), ones_rows], axis=2)
        cvt = cvt.reshape(b, N_KV_HEADS, cvt.shape[2], p_len // KEY_BLOCK, KEY_BLOCK)
        vt = jnp.concatenate([cvt.transpose(0, 1, 3, 2, 4), _pad_axis(vt, 4, KEY_BLOCK)], axis=2)
        kib = jnp.concatenate([cki.astype(BF16), kib], axis=1)
    ksel = min(TOPK_MAX, (p_len + l_true) // 4)
    tq = min(KEY_BLOCK, l)
    lk_pad = -(-(p_len + l) // KEY_BLOCK) * KEY_BLOCK
    assert p_len % KEY_BLOCK == 0 and (tq == KEY_BLOCK or l == tq)
    assert (p_len + l_true) % CHUNK == 0 and ksel <= KEY_BLOCK
    o = _attention(qt, qit, wi_t, _pad_axis(k_hm, 2, lk_pad), vt, _pad_axis(kib, 1, lk_pad),
                   tb, tri, tq=tq, past=p_len, ksel=ksel)

    prev = jnp.pad(conv_prev, ((0, 0), (CONV_HALO - (CONV_W - 1), 0), (0, 0)))
    y = _post(x, mod3, u, prev, sgc, o, sga, gm, w["conv_w"], w["conv_b"], w["ln_g"], w["ln_b"],
              w["w_conv_out"], w["w_attn_out"], w["w_out"], w["g_post"])
    conv_new = jnp.concatenate([conv_prev, u[:, :l_true]], axis=1)[:, -(CONV_W - 1):]
    return (y, conv_new, k[:, :l_true].reshape(b, l_true, N_KV_HEADS, HEAD_DIM),
            v[:, :l_true].reshape(b, l_true, N_KV_HEADS, HEAD_DIM), ki[:, :l_true])


def _pack_w_in(w_in, d_conv, d_attn, d_model):
    kvd = N_KV_HEADS * HEAD_DIM
    splits = (2 * d_conv, d_conv, d_attn, kvd, kvd, d_attn, IDX_HEADS * IDX_DIM, IDX_DIM,
              IDX_HEADS, 2 * d_model)
    offs = [0]
    for s in splits:
        offs.append(offs[-1] + s)
    seg = [w_in[:, offs[j]:offs[j + 1]] for j in range(len(splits))]
    kw = jnp.concatenate([seg[7], seg[8]], axis=1)
    kw = jnp.pad(kw, ((0, 0), (0, LANES - kw.shape[1])))
    return jnp.concatenate(seg[:7] + [kw, seg[9]], axis=1).astype(BF16)


def kernel(x_prompt, x_sample, c_prompt, c_sample, cache_k, cache_v, cache_kidx, state_conv,
           rel_bias, ada_w, ada_b, norm_pre, norm_post, w_in, conv_w, conv_b, conv_ln_g,
           conv_ln_b, w_conv_out, w_attn_out, w_out):
    depth = ada_w.shape[0]
    bp, _, d = x_prompt.shape
    d_conv = conv_w.shape[2]
    d_attn = N_HEADS * HEAD_DIM
    mod = _ada(jnp.concatenate([c_prompt, c_sample], axis=0), ada_w, ada_b)
    mod = mod.reshape(depth, mod.shape[1], 3, d)
    tb = _bias_tiles(rel_bias)
    tri = (jnp.arange(KEY_BLOCK)[:, None] >= jnp.arange(KEY_BLOCK)[None, :]).astype(BF16)

    lp, ls = x_prompt.shape[1], x_sample.shape[1]
    xp, xs = x_prompt, _pad_axis(x_sample, 1, -(-ls // LANES) * LANES)
    outs = [[] for _ in range(8)]
    for l in range(depth):
        w = dict(
            g_pre=norm_pre[l][None], g_post=norm_post[l][None],
            w_in=_pack_w_in(w_in[l], d_conv, d_attn, d),
            conv_w=conv_w[l], conv_b=conv_b[l][None], ln_g=conv_ln_g[l][None],
            ln_b=conv_ln_b[l][None], w_conv_out=w_conv_out[l].astype(BF16),
            w_attn_out=w_attn_out[l].astype(BF16), w_out=w_out[l].astype(BF16))
        zeros_conv = jnp.zeros((bp, CONV_W - 1, d_conv), xp.dtype)
        xp, cp, kp, vp, kip = _sublayer(xp, lp, mod[l, :bp], zeros_conv, None, tb, tri, w)
        xs, cs, kss, vss, kis = _sublayer(xs, ls, mod[l, bp:], state_conv[l],
                                          (cache_k[l], cache_v[l], cache_kidx[l]), tb, tri, w)
        for lst, val in zip(outs, (kp, vp, kip, cp, kss, vss, kis, cs)):
            lst.append(val)
    return (xp, xs[:, :ls]) + tuple(jnp.stack(o) for o in outs)
```

```python
import functools
import math

import jax
import jax.numpy as jnp
from jax import lax
from jax.experimental import pallas as pl
from jax.experimental.pallas import tpu as pltpu

F32 = jnp.float32
BF16 = jnp.bfloat16
I32 = jnp.int32
I16 = jnp.int16

CHUNK = 64
CONV_W = 31
N_HEADS = 16
N_KV_HEADS = 4
HEAD_DIM = 64
GROUP = N_HEADS // N_KV_HEADS
IDX_HEADS = 8
IDX_DIM = 64
TOPK_MAX = 256
N_BUCKETS = 32
MAX_DIST = 128
EPS = 1e-6

LANES = 128
SUBLANES = 8
PACKED_ROWS = 16
KEY_BLOCK = 256
BIAS_TILE = 128
CONV_HALO = 32
CONV_ROWS = 128
POST_ROWS = 256
VMEM_LIMIT_BYTES = 56 * 1024 * 1024
NEG_INF = float("-inf")
INT_MIN = -2 ** 31
HALF_MIN = -2 ** 15
LOG2E = 1.4426950408889634


def _sigmoid(x):
    return 1.0 / (1.0 + jnp.exp(-x))


def _silu(x):
    return x * _sigmoid(x)


def _params(*sem):
    return pltpu.CompilerParams(dimension_semantics=sem, vmem_limit_bytes=VMEM_LIMIT_BYTES)


def _ada_kernel(c_ref, w_ref, b_ref, o_ref):
    c = c_ref[...]
    o_ref[0] = jnp.dot(_silu(c), w_ref[0], preferred_element_type=F32) + b_ref[0]


def _ada(c_all, ada_w, ada_b):
    depth, d, _ = ada_w.shape
    bc = c_all.shape[0]
    return pl.pallas_call(
        _ada_kernel,
        grid=(depth, 3),
        in_specs=[
            pl.BlockSpec((bc, d), lambda l, n: (0, 0)),
            pl.BlockSpec((1, d, d), lambda l, n: (l, 0, n)),
            pl.BlockSpec((1, 1, d), lambda l, n: (l, 0, n)),
        ],
        out_specs=pl.BlockSpec((1, bc, d), lambda l, n: (l, 0, n)),
        out_shape=jax.ShapeDtypeStruct((depth, bc, 3 * d), F32),
        compiler_params=_params("arbitrary", "arbitrary"),
        name="ada",
    )(c_all, ada_w, ada_b.reshape(depth, 1, 3 * d))


def _bias_kernel(rb_ref, o_ref):
    r = lax.broadcasted_iota(I32, (BIAS_TILE, BIAS_TILE), 0)
    c = lax.broadcasted_iota(I32, (BIAS_TILE, BIAS_TILE), 1)
    nb = N_BUCKETS // 2
    max_exact = nb // 2
    for t in range(2):
        rel = (t - 1) * BIAS_TILE + r - c
        ret = jnp.where(rel > 0, nb, 0)
        n = jnp.abs(rel)
        nf = jnp.maximum(n, 1).astype(F32)
        large = max_exact + (jnp.log(nf / max_exact) / math.log(MAX_DIST / max_exact)
                             * (nb - max_exact)).astype(I32)
        large = jnp.minimum(large, nb - 1)
        bucket = ret + jnp.where(n < max_exact, n, large)
        for h in range(N_HEADS):
            acc = jnp.zeros((BIAS_TILE, BIAS_TILE), F32)
            for b in range(N_BUCKETS):
                acc = jnp.where(bucket == b, rb_ref[b, h], acc)
            o_ref[h, t] = acc * LOG2E


def _bias_tiles(rel_bias):
    return pl.pallas_call(
        _bias_kernel,
        in_specs=[pl.BlockSpec(memory_space=pltpu.SMEM)],
        out_specs=pl.BlockSpec(memory_space=pltpu.VMEM),
        out_shape=jax.ShapeDtypeStruct((N_HEADS, 2, BIAS_TILE, BIAS_TILE), F32),
        name="bias_tiles",
    )(rel_bias)


def _seg_offsets(d_conv, d_attn):
    widths = (2 * d_conv, d_conv, d_attn, N_KV_HEADS * HEAD_DIM, N_KV_HEADS * HEAD_DIM, d_attn,
              IDX_HEADS * IDX_DIM, LANES, None)
    offs, acc = [], 0
    for w in widths:
        offs.append(acc)
        if w is not None:
            acc += w
    return offs


def _inproj_kernel(x_ref, mod_ref, g_ref, w_ref,
                   u_ref, sgc_ref, qt_ref, k_ref, v_ref, kb_ref, vt_ref, sga_ref, qit_ref,
                   ki_ref, kib_ref, wi_ref, gm_ref, *, tm, d_conv, d_attn, d_model):
    x = x_ref[0]
    ms = jnp.mean(x * x, axis=-1, keepdims=True)
    y = x * lax.rsqrt(ms + EPS) * g_ref[...]
    shift = mod_ref[0, 0:1, :]
    scale = mod_ref[0, 1:2, :]
    h = (y * (1.0 + scale) + shift).astype(BF16)
    o_a, o_gc, o_q, o_k, o_v, o_ga, o_qi, o_kw, o_gm = _seg_offsets(d_conv, d_attn)

    def mm(lo, width):
        return jnp.dot(h, w_ref[:, lo:lo + width], preferred_element_type=F32)

    a = mm(o_a, 2 * d_conv)
    u_ref[0] = a[:, :d_conv] * _sigmoid(a[:, d_conv:])
    sgc_ref[0] = _silu(mm(o_gc, d_conv)).astype(BF16)
    zqt = (mm(o_q, d_attn) * (HEAD_DIM ** -0.5 * LOG2E)).T
    for hh in range(N_HEADS):
        g, j = divmod(hh, GROUP)
        qt_ref[0, g, 0, :, j * tm:(j + 1) * tm] = (
            zqt[hh * HEAD_DIM:(hh + 1) * HEAD_DIM].astype(BF16))
    zk = mm(o_k, N_KV_HEADS * HEAD_DIM)
    zv = mm(o_v, N_KV_HEADS * HEAD_DIM)
    k_ref[0] = zk
    v_ref[0] = zv
    zvt = zv.T
    ones_rows = jnp.where(lax.broadcasted_iota(I32, (PACKED_ROWS, tm), 0) == 0, 1.0, 0.0)
    for g in range(N_KV_HEADS):
        kb_ref[0, g] = zk[:, g * HEAD_DIM:(g + 1) * HEAD_DIM].astype(BF16)
        vt_ref[0, g, 0, 0:HEAD_DIM] = zvt[g * HEAD_DIM:(g + 1) * HEAD_DIM].astype(BF16)
        vt_ref[0, g, 0, HEAD_DIM:] = ones_rows.astype(BF16)
    sga_ref[0] = _silu(mm(o_ga, d_attn)).astype(BF16)
    zqit = (mm(o_qi, IDX_HEADS * IDX_DIM) * (IDX_DIM ** -0.5)).T
    for hh in range(IDX_HEADS):
        qit_ref[0, 0, :, hh * tm:(hh + 1) * tm] = zqit[hh * IDX_DIM:(hh + 1) * IDX_DIM].astype(BF16)
    zkw = mm(o_kw, LANES)
    zki = zkw[:, :IDX_DIM]
    ki_ref[0] = zki
    kib_ref[0] = zki.astype(BF16)
    wi_ref[0] = zkw.T[IDX_DIM:IDX_DIM + IDX_HEADS] * (IDX_HEADS ** -0.5)
    gm_ref[0] = _sigmoid(mm(o_gm, 2 * d_model)).astype(BF16)


def _inproj(x, mod3, g_pre, w_pad, *, d_conv, d_attn):
    b, l, d = x.shape
    tm = min(KEY_BLOCK, l)
    nw = w_pad.shape[1]
    kvd = N_KV_HEADS * HEAD_DIM

    def row(width):
        return pl.BlockSpec((1, tm, width), lambda bi, i: (bi, i, 0))

    def heads(n, width):
        return pl.BlockSpec((1, n, tm, width), lambda bi, i: (bi, 0, i, 0))

    nq = l // tm
    vrows = HEAD_DIM + PACKED_ROWS
    out_shape = (
        jax.ShapeDtypeStruct((b, l, d_conv), F32),
        jax.ShapeDtypeStruct((b, l, d_conv), BF16),
        jax.ShapeDtypeStruct((b, N_KV_HEADS, nq, HEAD_DIM, GROUP * tm), BF16),
        jax.ShapeDtypeStruct((b, l, kvd), F32),
        jax.ShapeDtypeStruct((b, l, kvd), F32),
        jax.ShapeDtypeStruct((b, N_KV_HEADS, l, HEAD_DIM), BF16),
        jax.ShapeDtypeStruct((b, N_KV_HEADS, nq, vrows, tm), BF16),
        jax.ShapeDtypeStruct((b, l, d_attn), BF16),
        jax.ShapeDtypeStruct((b, nq, IDX_DIM, IDX_HEADS * tm), BF16),
        jax.ShapeDtypeStruct((b, l, IDX_DIM), F32),
        jax.ShapeDtypeStruct((b, l, IDX_DIM), BF16),
        jax.ShapeDtypeStruct((b, IDX_HEADS, l), F32),
        jax.ShapeDtypeStruct((b, l, 2 * d), BF16),
    )
    out_specs = (row(d_conv), row(d_conv),
                 pl.BlockSpec((1, N_KV_HEADS, 1, HEAD_DIM, GROUP * tm), lambda bi, i: (bi, 0, i, 0, 0)),
                 row(kvd), row(kvd), heads(N_KV_HEADS, HEAD_DIM),
                 pl.BlockSpec((1, N_KV_HEADS, 1, vrows, tm), lambda bi, i: (bi, 0, i, 0, 0)),
                 row(d_attn),
                 pl.BlockSpec((1, 1, IDX_DIM, IDX_HEADS * tm), lambda bi, i: (bi, i, 0, 0)),
                 row(IDX_DIM), row(IDX_DIM),
                 pl.BlockSpec((1, IDX_HEADS, tm), lambda bi, i: (bi, 0, i)),
                 row(2 * d))
    return pl.pallas_call(
        functools.partial(_inproj_kernel, tm=tm, d_conv=d_conv, d_attn=d_attn, d_model=d),
        grid=(b, l // tm),
        in_specs=[
            pl.BlockSpec((1, tm, d), lambda bi, i: (bi, i, 0)),
            pl.BlockSpec((1, 3, d), lambda bi, i: (bi, 0, 0)),
            pl.BlockSpec((1, d), lambda bi, i: (0, 0)),
            pl.BlockSpec((d, nw), lambda bi, i: (0, 0), pipeline_mode=pl.Buffered(1)),
        ],
        out_specs=out_specs,
        out_shape=out_shape,
        compiler_params=_params("arbitrary", "arbitrary"),
        name="inproj",
    )(x, mod3, g_pre, w_pad)


def _attn_kernel(qt_ref, qit_ref, wi_ref, k_ref, vt_ref, ki_ref, tb_ref, tri_ref, o_ref,
                 key_scr, hi_scr, lo_scr, mb_scr, m_scr, acc_scr, ta_scr, sa_scr, tb_scr, sb_scr,
                 bias_scr, *, tq, past, ksel):
    i = pl.program_id(1)
    q0 = past + i * tq
    nkb = (q0 + tq + KEY_BLOCK - 1) // KEY_BLOCK

    def key_rows(kb):
        return pl.ds(pl.multiple_of(kb * KEY_BLOCK, KEY_BLOCK), KEY_BLOCK)

    qit = qit_ref[0, 0]
    wi = wi_ref[0]
    last = nkb - 1
    last_ok = (lax.broadcasted_iota(I32, (KEY_BLOCK, tq), 0) // CHUNK
               <= lax.broadcasted_iota(I32, (KEY_BLOCK, tq), 1) // CHUNK)

    def score_block(kb, is_last):
        ki_blk = ki_ref[0, key_rows(kb), :]
        acc = jnp.zeros((KEY_BLOCK, tq), F32)
        for h in range(IDX_HEADS):
            s = jnp.dot(ki_blk, qit[:, h * tq:(h + 1) * tq], preferred_element_type=F32)
            acc = acc + jnp.maximum(s, 0.0) * wi[h:h + 1, :]
        sc = jnp.where(last_ok, acc, NEG_INF) if is_last else acc
        bits = pltpu.bitcast(sc, I32)
        key = bits ^ ((bits >> 31) & 0x7FFFFFFF)
        key_scr[kb] = key
        half = pl.ds(pl.multiple_of((kb % 2) * KEY_BLOCK, KEY_BLOCK), KEY_BLOCK)
        hi_scr[kb // 2, half, :] = (key >> 16).astype(I16)
        lo_scr[kb // 2, half, :] = ((key & 0xFFFF) + HALF_MIN).astype(I16)

    def over_blocks(fn, init=0):
        carry = lax.fori_loop(0, last, lambda kb, c: fn(kb, False, c), init)
        return fn(last, True, carry)

    def score_pair(kp, carry):
        score_block(2 * kp, False)
        score_block(2 * kp + 1, False)
        return carry

    lax.fori_loop(0, last // 2, score_pair, 0)

    @pl.when(last % 2 == 1)
    def _():
        score_block(last - 1, False)

    score_block(last, True)

    npair = (nkb + 1) // 2

    @pl.when(nkb % 2 == 1)
    def _():
        floor16 = jnp.full((KEY_BLOCK, tq), HALF_MIN, I16)
        hi_scr[npair - 1, KEY_BLOCK:, :] = floor16
        lo_scr[npair - 1, KEY_BLOCK:, :] = floor16

    def count16(scr, pred, visit=None):
        groups = 2 * KEY_BLOCK // PACKED_ROWS

        def body(kp, acc):
            x = scr[kp]
            if visit is not None:
                visit(kp, x)
            hit = jnp.where(pred(x), jnp.int16(1), jnp.int16(0))
            hit = hit.reshape(groups, PACKED_ROWS, tq)
            parts = [hit[r] for r in range(groups)]
            while len(parts) > 1:
                parts = [parts[r] + parts[r + 1] for r in range(0, len(parts), 2)]
            return acc + parts[0]
        acc = lax.fori_loop(0, npair, body, jnp.zeros((PACKED_ROWS, tq), I16))
        return acc.astype(I32).sum(axis=0, keepdims=True)

    def search16(scr, target):
        def refine(it, state):
            thr, n_at = state
            cand = thr + jnp.left_shift(jnp.int32(1), 15 - it)
            c16 = cand.astype(I16)
            n = count16(scr, lambda x: x >= c16)
            return jnp.where(n >= target, cand, thr), jnp.where(n >= target, n, n_at)
        every = jnp.full((1, tq), 2 * KEY_BLOCK, I32) * npair
        return lax.fori_loop(0, 16, refine, (jnp.full((1, tq), HALF_MIN, I32), every))

    thr_hi, _ = search16(hi_scr, ksel)
    thr_hi16 = thr_hi.astype(I16)

    def restrict(kp, hi):
        lo_scr[kp] = jnp.where(hi == thr_hi16, lo_scr[kp], jnp.int16(HALF_MIN))

    above = count16(hi_scr, lambda x: x > thr_hi16, visit=restrict)
    thr_lo, n_lo = search16(lo_scr, ksel - above)
    thr = thr_hi * 65536 + (thr_lo - HALF_MIN)
    surplus = jnp.max(above + n_lo) > ksel

    @pl.when(jnp.logical_not(surplus))
    def _():
        def mask_block(kb, is_last, carry):
            keep = jnp.where(key_scr[kb] >= thr, 0.0, NEG_INF)
            mb_scr[kb] = (jnp.where(last_ok, keep, NEG_INF) if is_last else keep).astype(BF16)
            return carry

        over_blocks(mask_block)

    @pl.when(surplus)
    def _():
        def count_gt(kb, acc):
            hit = (key_scr[kb] > thr).astype(I32)
            return acc + hit.reshape(KEY_BLOCK // 8, 8, tq).sum(axis=0)

        n_gt = lax.fori_loop(0, nkb, count_gt, jnp.zeros((8, tq), I32)).sum(axis=0, keepdims=True)
        need = (ksel - n_gt).astype(F32)

        def mask_block(kb, is_last, ties_before):
            kk = key_scr[kb]
            eq = kk == thr
            rank = jnp.dot(tri_ref[...], jnp.where(eq, 1.0, 0.0).astype(BF16),
                           preferred_element_type=F32) + ties_before
            tie = jnp.where(eq, jnp.where(rank <= need, 0.0, NEG_INF), NEG_INF)
            keep = jnp.where(kk > thr, 0.0, tie)
            mb_scr[kb] = (jnp.where(last_ok, keep, NEG_INF) if is_last else keep).astype(BF16)
            return rank[KEY_BLOCK - 1:KEY_BLOCK, :]

        over_blocks(mask_block, jnp.zeros((1, tq), F32))

    nsub = tq // BIAS_TILE

    def bias_block(h, diag):
        same = tb_ref[h, 1]
        prev = tb_ref[h, 0]
        far = jnp.broadcast_to(tb_ref[h, 0, 0:1, 0:1], (BIAS_TILE, BIAS_TILE))
        if diag:
            grid = [[same, prev], [far, same]]
        else:
            grid = [[far, far], [prev, far]]
        return jnp.concatenate([jnp.concatenate(r[:nsub], axis=1) for r in grid], axis=0)

    @pl.when(jnp.logical_and(pl.program_id(0) == 0, i == 0))
    def _():
        for h in range(N_HEADS):
            bias_scr[0, h] = bias_block(h, False).astype(BF16)
            bias_scr[1, h] = bias_block(h, True).astype(BF16)

    def produce(h, kb, kind, buf, gated=False):
        t_buf, s_buf = buf
        g, j = divmod(h, GROUP)
        lanes = slice(j * tq, (j + 1) * tq)
        t = jnp.dot(k_ref[0, g, key_rows(kb), :], qt_ref[0, g, 0, :, lanes],
                    preferred_element_type=F32).astype(BF16) + mb_scr[kb]
        if kind == "far":
            shift = jnp.broadcast_to(tb_ref[h, 0, 0:1, 0:1], (1, tq))
        else:
            shift = jnp.zeros((1, tq), F32)
            t = t + bias_scr[int(kind == "diag"), h]
            if gated:
                t = t + jnp.where(nkb >= 2, 0.0, NEG_INF).astype(BF16)
        t_buf[g, :, lanes] = t
        s_buf[g, 0:1, lanes] = jnp.max(t, axis=0, keepdims=True).astype(F32) + shift
        s_buf[g, 1:2, lanes] = shift

    def consume(h, kb, buf):
        t_buf, s_buf = buf
        g, j = divmod(h, GROUP)
        lanes = slice(j * tq, (j + 1) * tq)
        m = m_scr[g, :, lanes]
        shift = s_buf[g, 1:2, lanes]
        m_new = jnp.maximum(m, s_buf[g, 0:1, lanes])
        sub = (jnp.where(m_new == NEG_INF, 0.0, m_new) - shift).astype(BF16)
        m_used = sub.astype(F32) + shift
        p = jnp.exp2(t_buf[g, :, lanes] - sub)
        acc_scr[g, :, lanes] = jnp.exp2(m - m_used) * acc_scr[g, :, lanes] + jnp.dot(
            vt_ref[0, g, kb], p, preferred_element_type=F32)
        m_scr[g, :, lanes] = jnp.where(m_new == NEG_INF, NEG_INF, m_used)

    def stage(new=None, old=None):
        for h in range(N_HEADS):
            if new is not None:
                produce(h, *new)
            if old is not None:
                consume(h, *old)

    m_scr[...] = jnp.full(m_scr.shape, NEG_INF, F32)
    acc_scr[...] = jnp.zeros(acc_scr.shape, F32)

    buf_a, buf_b = (ta_scr, sa_scr), (tb_scr, sb_scr)
    n_far = jnp.maximum(nkb - 2, 0)
    off = jnp.maximum(nkb - 2, 0)
    stage(new=(last, "diag", buf_a))
    stage(new=(off, "off", buf_b, True), old=(last, buf_a))

    def far_pair(it, carry):
        f = nkb - 3 - 2 * it
        stage(new=(f, "far", buf_a), old=(f + 1, buf_b))
        stage(new=(f - 1, "far", buf_b), old=(f, buf_a))
        return carry

    lax.fori_loop(0, n_far // 2, far_pair, 0)

    @pl.when(n_far % 2 == 1)
    def _():
        stage(new=(0, "far", buf_a), old=(1, buf_b))
        stage(old=(0, buf_a))

    @pl.when(n_far % 2 == 0)
    def _():
        stage(old=(jnp.maximum(nkb - 2 - n_far, 0), buf_b))

    heads_t = []
    for g in range(N_KV_HEADS):
        acc = acc_scr[g]
        o = acc[:HEAD_DIM] / acc[HEAD_DIM:HEAD_DIM + 1]
        heads_t += [o[:, j * tq:(j + 1) * tq] for j in range(GROUP)]
    o_ref[0] = jnp.concatenate(heads_t, axis=0).T.astype(o_ref.dtype)


def _attention(qt, qit, wi_t, k_hm, vt, kib, tb, tri, *, tq, past, ksel):
    b, _, nq, _, _ = qt.shape
    lk = k_hm.shape[2]
    nkb_max = lk // KEY_BLOCK
    blocks = (nkb_max, KEY_BLOCK, tq)
    pairs = ((nkb_max + 1) // 2, 2 * KEY_BLOCK, tq)
    return pl.pallas_call(
        functools.partial(_attn_kernel, tq=tq, past=past, ksel=ksel),
        grid=(b, nq),
        in_specs=[
            pl.BlockSpec((1, N_KV_HEADS, 1, HEAD_DIM, GROUP * tq), lambda bi, i: (bi, 0, i, 0, 0)),
            pl.BlockSpec((1, 1, IDX_DIM, IDX_HEADS * tq), lambda bi, i: (bi, i, 0, 0)),
            pl.BlockSpec((1, IDX_HEADS, tq), lambda bi, i: (bi, 0, i)),
            pl.BlockSpec((1, N_KV_HEADS, lk, HEAD_DIM), lambda bi, i: (bi, 0, 0, 0)),
            pl.BlockSpec((1, N_KV_HEADS, nkb_max, vt.shape[3], KEY_BLOCK),
                         lambda bi, i: (bi, 0, 0, 0, 0)),
            pl.BlockSpec((1, lk, IDX_DIM), lambda bi, i: (bi, 0, 0)),
            pl.BlockSpec((N_HEADS, 2, BIAS_TILE, BIAS_TILE), lambda bi, i: (0, 0, 0, 0)),
            pl.BlockSpec((KEY_BLOCK, KEY_BLOCK), lambda bi, i: (0, 0)),
        ],
        out_specs=pl.BlockSpec((1, tq, N_HEADS * HEAD_DIM), lambda bi, i: (bi, i, 0)),
        out_shape=jax.ShapeDtypeStruct((b, nq * tq, N_HEADS * HEAD_DIM), BF16),
        scratch_shapes=[pltpu.VMEM(blocks, I32), pltpu.VMEM(pairs, I16), pltpu.VMEM(pairs, I16),
                        pltpu.VMEM(blocks, BF16),
                        pltpu.VMEM((N_KV_HEADS, 1, GROUP * tq), F32),
                        pltpu.VMEM((N_KV_HEADS, vt.shape[3], GROUP * tq), F32)]
        + 2 * [pltpu.VMEM((N_KV_HEADS, KEY_BLOCK, GROUP * tq), BF16),
               pltpu.VMEM((N_KV_HEADS, 2, GROUP * tq), F32)]
        + [pltpu.VMEM((2, N_HEADS, KEY_BLOCK, tq), BF16)],
        compiler_params=_params("arbitrary", "arbitrary"),
        name="attention",
    )(qt, qit, wi_t, k_hm, vt, kib, tb, tri)


def _post_kernel(x_ref, mod_ref, u_ref, halo_ref, prev_ref, sgc_ref, o_ref, sga_ref, gm_ref,
                 cw_ref, cb_ref, lg_ref, lb_ref, wc_ref, wa_ref, wo_ref, gp_ref,
                 y_ref, up_scr, sh_scr, cv_scr, *, tt, d_conv, d_model):
    i = pl.program_id(1)
    up_scr[0:CONV_HALO] = jnp.where(i == 0, prev_ref[0], halo_ref[0])
    up_scr[CONV_HALO:] = u_ref[0]
    span = tt + CONV_HALO - SUBLANES
    for s in range(1, SUBLANES):
        sh_scr[s - 1] = up_scr[pl.ds(s, span), :]
    pad = CONV_HALO - (CONV_W - 1)
    rt = min(tt, CONV_ROWS)
    for r0 in range(0, tt, rt):
        for c0 in range(0, d_conv, LANES):
            acc = jnp.broadcast_to(cb_ref[:, c0:c0 + LANES], (rt, LANES))
            for j in range(CONV_W):
                a, s = divmod(pad + j, SUBLANES)
                rows = pl.ds(r0 + SUBLANES * a, rt)
                win = up_scr[rows, c0:c0 + LANES] if s == 0 else sh_scr[s - 1, rows, c0:c0 + LANES]
                acc = acc + cw_ref[j:j + 1, c0:c0 + LANES] * win
            cv_scr[r0:r0 + rt, c0:c0 + LANES] = acc
    cv = cv_scr[...]
    mu = jnp.mean(cv, axis=-1, keepdims=True)
    dv = cv - mu
    var = jnp.mean(dv * dv, axis=-1, keepdims=True)
    yc = _silu(dv * lax.rsqrt(var + EPS) * lg_ref[...] + lb_ref[...])
    yc = (yc * sgc_ref[0].astype(F32)).astype(BF16)
    y_c = jnp.dot(yc, wc_ref[...], preferred_element_type=F32)
    ya = (o_ref[0].astype(F32) * sga_ref[0].astype(F32)).astype(BF16)
    y_a = jnp.dot(ya, wa_ref[...], preferred_element_type=F32)
    gm = gm_ref[0].astype(F32)
    mix = (gm[:, :d_model] * y_c + gm[:, d_model:] * y_a).astype(BF16)
    y = jnp.dot(mix, wo_ref[...], preferred_element_type=F32)
    y = y * lax.rsqrt(jnp.mean(y * y, axis=-1, keepdims=True) + EPS) * gp_ref[...]
    y_ref[0] = x_ref[0] + mod_ref[0, 2:3, :] * y


def _post(x, mod3, u, prev, sgc, o, sga, gm, conv_w, conv_b, ln_g, ln_b, wc, wa, wo, g_post):
    b, l, d = x.shape
    d_conv = u.shape[2]
    d_attn = o.shape[2]
    tt = min(POST_ROWS, l)
    hb = tt // CONV_HALO

    def row(width):
        return pl.BlockSpec((1, tt, width), lambda bi, i: (bi, i, 0))

    def const(shape):
        return pl.BlockSpec(shape, lambda bi, i: (0,) * len(shape))

    return pl.pallas_call(
        functools.partial(_post_kernel, tt=tt, d_conv=d_conv, d_model=d),
        grid=(b, l // tt),
        in_specs=[
            row(d),
            pl.BlockSpec((1, 3, d), lambda bi, i: (bi, 0, 0)),
            row(d_conv),
            pl.BlockSpec((1, CONV_HALO, d_conv), lambda bi, i: (bi, jnp.maximum(i * hb - 1, 0), 0)),
            pl.BlockSpec((1, CONV_HALO, d_conv), lambda bi, i: (bi, 0, 0)),
            row(d_conv), row(d_attn), row(d_attn), row(2 * d),
            const((CONV_W, d_conv)), const((1, d_conv)), const((1, d_conv)), const((1, d_conv)),
            const((d_conv, d)), const((d_attn, d)), const((d, d)), const((1, d)),
        ],
        out_specs=row(d),
        out_shape=jax.ShapeDtypeStruct((b, l, d), F32),
        scratch_shapes=[pltpu.VMEM((tt + CONV_HALO, d_conv), F32),
                        pltpu.VMEM((SUBLANES - 1, tt + CONV_HALO - SUBLANES, d_conv), F32),
                        pltpu.VMEM((tt, d_conv), F32)],
        compiler_params=_params("arbitrary", "arbitrary"),
        name="post",
    )(x, mod3, u, u, prev, sgc, o, sga, gm, conv_w, conv_b, ln_g, ln_b, wc, wa, wo, g_post)


def _pad_axis(t, axis, size):
    if t.shape[axis] == size:
        return t
    widths = [(0, 0)] * t.ndim
    widths[axis] = (0, size - t.shape[axis])
    return jnp.pad(t, widths)


def _sublayer(x, l_true, mod3, conv_prev, past, tb, tri, w):
    b, l, d = x.shape
    d_conv = w["conv_w"].shape[1]
    d_attn = N_HEADS * HEAD_DIM
    (u, sgc, qt, k, v, k_hm, vt, sga, qit, ki, kib, wi_t, gm) = _inproj(
        x, mod3, w["g_pre"], w["w_in"], d_conv=d_conv, d_attn=d_attn)

    if past is None:
        p_len = 0
    else:
        ck, cv, cki = past
        p_len = ck.shape[1]
        k_hm = jnp.concatenate([ck.transpose(0, 2, 1, 3).astype(BF16), k_hm], axis=2)
        ones_rows = jnp.zeros((b, N_KV_HEADS, PACKED_ROWS, p_len), BF16).at[:, :, 0].set(1.0)
        cvt = jnp.concatenate([cv.transpose(0, 2, 3, 1).astype(BF16), ones_rows], axis=2)
        cvt = cvt.reshape(b, N_KV_HEADS, cvt.shape[2], p_len // KEY_BLOCK, KEY_BLOCK)
        vt = jnp.concatenate([cvt.transpose(0, 1, 3, 2, 4), _pad_axis(vt, 4, KEY_BLOCK)], axis=2)
        kib = jnp.concatenate([cki.astype(BF16), kib], axis=1)
    ksel = min(TOPK_MAX, (p_len + l_true) // 4)
    tq = min(KEY_BLOCK, l)
    lk_pad = -(-(p_len + l) // KEY_BLOCK) * KEY_BLOCK
    assert p_len % KEY_BLOCK == 0 and (tq == KEY_BLOCK or l == tq)
    assert (p_len + l_true) % CHUNK == 0 and ksel <= KEY_BLOCK
    o = _attention(qt, qit, wi_t, _pad_axis(k_hm, 2, lk_pad), vt, _pad_axis(kib, 1, lk_pad),
                   tb, tri, tq=tq, past=p_len, ksel=ksel)

    prev = jnp.pad(conv_prev, ((0, 0), (CONV_HALO - (CONV_W - 1), 0), (0, 0)))
    y = _post(x, mod3, u, prev, sgc, o, sga, gm, w["conv_w"], w["conv_b"], w["ln_g"], w["ln_b"],
              w["w_conv_out"], w["w_attn_out"], w["w_out"], w["g_post"])
    conv_new = jnp.concatenate([conv_prev, u[:, :l_true]], axis=1)[:, -(CONV_W - 1):]
    return (y, conv_new, k[:, :l_true].reshape(b, l_true, N_KV_HEADS, HEAD_DIM),
            v[:, :l_true].reshape(b, l_true, N_KV_HEADS, HEAD_DIM), ki[:, :l_true])


def _pack_w_in(w_in, d_conv, d_attn, d_model):
    kvd = N_KV_HEADS * HEAD_DIM
    splits = (2 * d_conv, d_conv, d_attn, kvd, kvd, d_attn, IDX_HEADS * IDX_DIM, IDX_DIM,
              IDX_HEADS, 2 * d_model)
    offs = [0]
    for s in splits:
        offs.append(offs[-1] + s)
    seg = [w_in[:, offs[j]:offs[j + 1]] for j in range(len(splits))]
    kw = jnp.concatenate([seg[7], seg[8]], axis=1)
    kw = jnp.pad(kw, ((0, 0), (0, LANES - kw.shape[1])))
    return jnp.concatenate(seg[:7] + [kw, seg[9]], axis=1).astype(BF16)


def kernel(x_prompt, x_sample, c_prompt, c_sample, cache_k, cache_v, cache_kidx, state_conv,
           rel_bias, ada_w, ada_b, norm_pre, norm_post, w_in, conv_w, conv_b, conv_ln_g,
           conv_ln_b, w_conv_out, w_attn_out, w_out):
    depth = ada_w.shape[0]
    bp, _, d = x_prompt.shape
    d_conv = conv_w.shape[2]
    d_attn = N_HEADS * HEAD_DIM
    mod = _ada(jnp.concatenate([c_prompt, c_sample], axis=0), ada_w, ada_b)
    mod = mod.reshape(depth, mod.shape[1], 3, d)
    tb = _bias_tiles(rel_bias)
    tri = (jnp.arange(KEY_BLOCK)[:, None] >= jnp.arange(KEY_BLOCK)[None, :]).astype(BF16)

    lp, ls = x_prompt.shape[1], x_sample.shape[1]
    xp, xs = x_prompt, _pad_axis(x_sample, 1, -(-ls // LANES) * LANES)
    outs = [[] for _ in range(8)]
    for l in range(depth):
        w = dict(
            g_pre=norm_pre[l][None], g_post=norm_post[l][None],
            w_in=_pack_w_in(w_in[l], d_conv, d_attn, d),
            conv_w=conv_w[l], conv_b=conv_b[l][None], ln_g=conv_ln_g[l][None],
            ln_b=conv_ln_b[l][None], w_conv_out=w_conv_out[l].astype(BF16),
            w_attn_out=w_attn_out[l].astype(BF16), w_out=w_out[l].astype(BF16))
        zeros_conv = jnp.zeros((bp, CONV_W - 1, d_conv), xp.dtype)
        xp, cp, kp, vp, kip = _sublayer(xp, lp, mod[l, :bp], zeros_conv, None, tb, tri, w)
        xs, cs, kss, vss, kis = _sublayer(xs, ls, mod[l, bp:], state_conv[l],
                                          (cache_k[l], cache_v[l], cache_kidx[l]), tb, tri, w)
        for lst, val in zip(outs, (kp, vp, kip, cp, kss, vss, kis, cs)):
            lst.append(val)
    return (xp, xs[:, :ls]) + tuple(jnp.stack(o) for o in outs)
```

```python
import functools
import math

import jax
import jax.numpy as jnp
from jax import lax
from jax.experimental import pallas as pl
from jax.experimental.pallas import tpu as pltpu

F32 = jnp.float32
BF16 = jnp.bfloat16
I32 = jnp.int32
I16 = jnp.int16

CHUNK = 64
CONV_W = 31
N_HEADS = 16
N_KV_HEADS = 4
HEAD_DIM = 64
GROUP = N_HEADS // N_KV_HEADS
IDX_HEADS = 8
IDX_DIM = 64
TOPK_MAX = 256
N_BUCKETS = 32
MAX_DIST = 128
EPS = 1e-6

LANES = 128
SUBLANES = 8
PACKED_ROWS = 16
KEY_BLOCK = 256
QK_DEPTH = 128
BIAS_TILE = 128
CONV_HALO = 32
CONV_ROWS = 128
POST_ROWS = 256
VMEM_LIMIT_BYTES = 56 * 1024 * 1024
NEG_INF = float("-inf")
INT_MIN = -2 ** 31
HALF_MIN = -2 ** 15
LOG2E = 1.4426950408889634


def _sigmoid(x):
    return 1.0 / (1.0 + jnp.exp(-x))


def _silu(x):
    return x * _sigmoid(x)


def _params(*sem):
    return pltpu.CompilerParams(dimension_semantics=sem, vmem_limit_bytes=VMEM_LIMIT_BYTES)


def _ada_kernel(c_ref, w_ref, b_ref, o_ref):
    c = c_ref[...]
    o_ref[0] = jnp.dot(_silu(c), w_ref[0], preferred_element_type=F32) + b_ref[0]


def _ada(c_all, ada_w, ada_b):
    depth, d, _ = ada_w.shape
    bc = c_all.shape[0]
    return pl.pallas_call(
        _ada_kernel,
        grid=(depth, 3),
        in_specs=[
            pl.BlockSpec((bc, d), lambda l, n: (0, 0)),
            pl.BlockSpec((1, d, d), lambda l, n: (l, 0, n)),
            pl.BlockSpec((1, 1, d), lambda l, n: (l, 0, n)),
        ],
        out_specs=pl.BlockSpec((1, bc, d), lambda l, n: (l, 0, n)),
        out_shape=jax.ShapeDtypeStruct((depth, bc, 3 * d), F32),
        compiler_params=_params("arbitrary", "arbitrary"),
        name="ada",
    )(c_all, ada_w, ada_b.reshape(depth, 1, 3 * d))


def _bias_kernel(rb_ref, o_ref):
    r = lax.broadcasted_iota(I32, (BIAS_TILE, BIAS_TILE), 0)
    c = lax.broadcasted_iota(I32, (BIAS_TILE, BIAS_TILE), 1)
    nb = N_BUCKETS // 2
    max_exact = nb // 2
    for t in range(2):
        rel = (t - 1) * BIAS_TILE + r - c
        ret = jnp.where(rel > 0, nb, 0)
        n = jnp.abs(rel)
        nf = jnp.maximum(n, 1).astype(F32)
        large = max_exact + (jnp.log(nf / max_exact) / math.log(MAX_DIST / max_exact)
                             * (nb - max_exact)).astype(I32)
        large = jnp.minimum(large, nb - 1)
        bucket = ret + jnp.where(n < max_exact, n, large)
        for h in range(N_HEADS):
            acc = jnp.zeros((BIAS_TILE, BIAS_TILE), F32)
            for b in range(N_BUCKETS):
                acc = jnp.where(bucket == b, rb_ref[b, h], acc)
            o_ref[h, t] = acc * LOG2E


def _bias_tiles(rel_bias):
    return pl.pallas_call(
        _bias_kernel,
        in_specs=[pl.BlockSpec(memory_space=pltpu.SMEM)],
        out_specs=pl.BlockSpec(memory_space=pltpu.VMEM),
        out_shape=jax.ShapeDtypeStruct((N_HEADS, 2, BIAS_TILE, BIAS_TILE), F32),
        name="bias_tiles",
    )(rel_bias)


def _seg_offsets(d_conv, d_attn):
    widths = (2 * d_conv, d_conv, d_attn, N_KV_HEADS * HEAD_DIM, N_KV_HEADS * HEAD_DIM, d_attn,
              IDX_HEADS * IDX_DIM, LANES, None)
    offs, acc = [], 0
    for w in widths:
        offs.append(acc)
        if w is not None:
            acc += w
    return offs


def _inproj_kernel(x_ref, mod_ref, g_ref, w_ref,
                   u_ref, sgc_ref, qt_ref, k_ref, v_ref, kb_ref, vt_ref, sga_ref, qit_ref,
                   ki_ref, kib_ref, wi_ref, gm_ref, *, tm, d_conv, d_attn, d_model):
    x = x_ref[0]
    ms = jnp.mean(x * x, axis=-1, keepdims=True)
    y = x * lax.rsqrt(ms + EPS) * g_ref[...]
    shift = mod_ref[0, 0:1, :]
    scale = mod_ref[0, 1:2, :]
    h = (y * (1.0 + scale) + shift).astype(BF16)
    o_a, o_gc, o_q, o_k, o_v, o_ga, o_qi, o_kw, o_gm = _seg_offsets(d_conv, d_attn)

    def mm(lo, width):
        return jnp.dot(h, w_ref[:, lo:lo + width], preferred_element_type=F32)

    a = mm(o_a, 2 * d_conv)
    u_ref[0] = a[:, :d_conv] * _sigmoid(a[:, d_conv:])
    sgc_ref[0] = _silu(mm(o_gc, d_conv)).astype(BF16)
    zqt = (mm(o_q, d_attn) * (HEAD_DIM ** -0.5 * LOG2E)).T
    for hh in range(N_HEADS):
        g, j = divmod(hh, GROUP)
        qt_ref[0, g, 0, 0:HEAD_DIM, j * tm:(j + 1) * tm] = (
            zqt[hh * HEAD_DIM:(hh + 1) * HEAD_DIM].astype(BF16))
    qt_ref[0, :, 0, HEAD_DIM:, :] = jnp.zeros((N_KV_HEADS, QK_DEPTH - HEAD_DIM, GROUP * tm), BF16)
    lane_pad = jnp.zeros((tm, QK_DEPTH - HEAD_DIM), BF16)
    zk = mm(o_k, N_KV_HEADS * HEAD_DIM)
    zv = mm(o_v, N_KV_HEADS * HEAD_DIM)
    k_ref[0] = zk
    v_ref[0] = zv
    zvt = zv.T
    ones_rows = jnp.where(lax.broadcasted_iota(I32, (PACKED_ROWS, tm), 0) == 0, 1.0, 0.0)
    for g in range(N_KV_HEADS):
        kb_ref[0, g] = jnp.concatenate(
            [zk[:, g * HEAD_DIM:(g + 1) * HEAD_DIM].astype(BF16), lane_pad], axis=1)
        vt_ref[0, g, 0, 0:HEAD_DIM] = zvt[g * HEAD_DIM:(g + 1) * HEAD_DIM].astype(BF16)
        vt_ref[0, g, 0, HEAD_DIM:] = ones_rows.astype(BF16)
    sga_ref[0] = _silu(mm(o_ga, d_attn)).astype(BF16)
    zqit = (mm(o_qi, IDX_HEADS * IDX_DIM) * (IDX_DIM ** -0.5)).T
    for hh in range(IDX_HEADS):
        qit_ref[0, 0, 0:IDX_DIM, hh * tm:(hh + 1) * tm] = (
            zqit[hh * IDX_DIM:(hh + 1) * IDX_DIM].astype(BF16))
    qit_ref[0, 0, IDX_DIM:, :] = jnp.zeros((QK_DEPTH - IDX_DIM, IDX_HEADS * tm), BF16)
    zkw = mm(o_kw, LANES)
    zki = zkw[:, :IDX_DIM]
    ki_ref[0] = zki
    kib_ref[0] = jnp.concatenate([zki.astype(BF16), lane_pad], axis=1)
    wi_ref[0] = zkw.T[IDX_DIM:IDX_DIM + IDX_HEADS] * (IDX_HEADS ** -0.5)
    gm_ref[0] = _sigmoid(mm(o_gm, 2 * d_model)).astype(BF16)


def _inproj(x, mod3, g_pre, w_pad, *, d_conv, d_attn):
    b, l, d = x.shape
    tm = min(KEY_BLOCK, l)
    nw = w_pad.shape[1]
    kvd = N_KV_HEADS * HEAD_DIM

    def row(width):
        return pl.BlockSpec((1, tm, width), lambda bi, i: (bi, i, 0))

    def heads(n, width):
        return pl.BlockSpec((1, n, tm, width), lambda bi, i: (bi, 0, i, 0))

    nq = l // tm
    vrows = HEAD_DIM + PACKED_ROWS
    out_shape = (
        jax.ShapeDtypeStruct((b, l, d_conv), F32),
        jax.ShapeDtypeStruct((b, l, d_conv), BF16),
        jax.ShapeDtypeStruct((b, N_KV_HEADS, nq, QK_DEPTH, GROUP * tm), BF16),
        jax.ShapeDtypeStruct((b, l, kvd), F32),
        jax.ShapeDtypeStruct((b, l, kvd), F32),
        jax.ShapeDtypeStruct((b, N_KV_HEADS, l, QK_DEPTH), BF16),
        jax.ShapeDtypeStruct((b, N_KV_HEADS, nq, vrows, tm), BF16),
        jax.ShapeDtypeStruct((b, l, d_attn), BF16),
        jax.ShapeDtypeStruct((b, nq, QK_DEPTH, IDX_HEADS * tm), BF16),
        jax.ShapeDtypeStruct((b, l, IDX_DIM), F32),
        jax.ShapeDtypeStruct((b, l, QK_DEPTH), BF16),
        jax.ShapeDtypeStruct((b, IDX_HEADS, l), F32),
        jax.ShapeDtypeStruct((b, l, 2 * d), BF16),
    )
    out_specs = (row(d_conv), row(d_conv),
                 pl.BlockSpec((1, N_KV_HEADS, 1, QK_DEPTH, GROUP * tm), lambda bi, i: (bi, 0, i, 0, 0)),
                 row(kvd), row(kvd), heads(N_KV_HEADS, QK_DEPTH),
                 pl.BlockSpec((1, N_KV_HEADS, 1, vrows, tm), lambda bi, i: (bi, 0, i, 0, 0)),
                 row(d_attn),
                 pl.BlockSpec((1, 1, QK_DEPTH, IDX_HEADS * tm), lambda bi, i: (bi, i, 0, 0)),
                 row(IDX_DIM), row(QK_DEPTH),
                 pl.BlockSpec((1, IDX_HEADS, tm), lambda bi, i: (bi, 0, i)),
                 row(2 * d))
    return pl.pallas_call(
        functools.partial(_inproj_kernel, tm=tm, d_conv=d_conv, d_attn=d_attn, d_model=d),
        grid=(b, l // tm),
        in_specs=[
            pl.BlockSpec((1, tm, d), lambda bi, i: (bi, i, 0)),
            pl.BlockSpec((1, 3, d), lambda bi, i: (bi, 0, 0)),
            pl.BlockSpec((1, d), lambda bi, i: (0, 0)),
            pl.BlockSpec((d, nw), lambda bi, i: (0, 0), pipeline_mode=pl.Buffered(1)),
        ],
        out_specs=out_specs,
        out_shape=out_shape,
        compiler_params=_params("arbitrary", "arbitrary"),
        name="inproj",
    )(x, mod3, g_pre, w_pad)


def _attn_kernel(qt_ref, qit_ref, wi_ref, k_ref, vt_ref, ki_ref, tb_ref, tri_ref, o_ref,
                 key_scr, hi_scr, lo_scr, mb_scr, m_scr, acc_scr, ta_scr, sa_scr, tb_scr, sb_scr,
                 bias_scr, *, tq, past, ksel):
    i = pl.program_id(1)
    q0 = past + i * tq
    nkb = (q0 + tq + KEY_BLOCK - 1) // KEY_BLOCK

    def key_rows(kb):
        return pl.ds(pl.multiple_of(kb * KEY_BLOCK, KEY_BLOCK), KEY_BLOCK)

    qit = qit_ref[0, 0]
    wi = wi_ref[0]
    last = nkb - 1
    last_ok = (lax.broadcasted_iota(I32, (KEY_BLOCK, tq), 0) // CHUNK
               <= lax.broadcasted_iota(I32, (KEY_BLOCK, tq), 1) // CHUNK)

    def score_block(kb, is_last):
        ki_blk = ki_ref[0, key_rows(kb), :]
        acc = jnp.zeros((KEY_BLOCK, tq), F32)
        for h in range(IDX_HEADS):
            s = jnp.dot(ki_blk, qit[:, h * tq:(h + 1) * tq], preferred_element_type=F32)
            acc = acc + jnp.maximum(s, 0.0) * wi[h:h + 1, :]
        sc = jnp.where(last_ok, acc, NEG_INF) if is_last else acc
        bits = pltpu.bitcast(sc, I32)
        key = bits ^ ((bits >> 31) & 0x7FFFFFFF)
        key_scr[kb] = key
        half = pl.ds(pl.multiple_of((kb % 2) * KEY_BLOCK, KEY_BLOCK), KEY_BLOCK)
        hi_scr[kb // 2, half, :] = (key >> 16).astype(I16)
        lo_scr[kb // 2, half, :] = ((key & 0xFFFF) + HALF_MIN).astype(I16)

    def over_blocks(fn, init=0):
        carry = lax.fori_loop(0, last, lambda kb, c: fn(kb, False, c), init)
        return fn(last, True, carry)

    def score_pair(kp, carry):
        score_block(2 * kp, False)
        score_block(2 * kp + 1, False)
        return carry

    lax.fori_loop(0, last // 2, score_pair, 0)

    @pl.when(last % 2 == 1)
    def _():
        score_block(last - 1, False)

    score_block(last, True)

    npair = (nkb + 1) // 2

    @pl.when(nkb % 2 == 1)
    def _():
        floor16 = jnp.full((KEY_BLOCK, tq), HALF_MIN, I16)
        hi_scr[npair - 1, KEY_BLOCK:, :] = floor16
        lo_scr[npair - 1, KEY_BLOCK:, :] = floor16

    def count16(scr, pred, visit=None):
        groups = 2 * KEY_BLOCK // PACKED_ROWS

        def body(kp, acc):
            x = scr[kp]
            if visit is not None:
                visit(kp, x)
            hit = jnp.where(pred(x), jnp.int16(1), jnp.int16(0))
            hit = hit.reshape(groups, PACKED_ROWS, tq)
            parts = [hit[r] for r in range(groups)]
            while len(parts) > 1:
                parts = [parts[r] + parts[r + 1] for r in range(0, len(parts), 2)]
            return acc + parts[0]
        acc = lax.fori_loop(0, npair, body, jnp.zeros((PACKED_ROWS, tq), I16))
        return acc.astype(I32).sum(axis=0, keepdims=True)

    def search16(scr, target):
        def refine(it, state):
            thr, n_at = state
            cand = thr + jnp.left_shift(jnp.int32(1), 15 - it)
            c16 = cand.astype(I16)
            n = count16(scr, lambda x: x >= c16)
            return jnp.where(n >= target, cand, thr), jnp.where(n >= target, n, n_at)
        every = jnp.full((1, tq), 2 * KEY_BLOCK, I32) * npair
        return lax.fori_loop(0, 16, refine, (jnp.full((1, tq), HALF_MIN, I32), every))

    thr_hi, _ = search16(hi_scr, ksel)
    thr_hi16 = thr_hi.astype(I16)

    def restrict(kp, hi):
        lo_scr[kp] = jnp.where(hi == thr_hi16, lo_scr[kp], jnp.int16(HALF_MIN))

    above = count16(hi_scr, lambda x: x > thr_hi16, visit=restrict)
    thr_lo, n_lo = search16(lo_scr, ksel - above)
    thr = thr_hi * 65536 + (thr_lo - HALF_MIN)
    surplus = jnp.max(above + n_lo) > ksel

    @pl.when(jnp.logical_not(surplus))
    def _():
        def mask_block(kb, is_last, carry):
            keep = jnp.where(key_scr[kb] >= thr, 0.0, NEG_INF)
            mb_scr[kb] = (jnp.where(last_ok, keep, NEG_INF) if is_last else keep).astype(BF16)
            return carry

        over_blocks(mask_block)

    @pl.when(surplus)
    def _():
        def count_gt(kb, acc):
            hit = (key_scr[kb] > thr).astype(I32)
            return acc + hit.reshape(KEY_BLOCK // 8, 8, tq).sum(axis=0)

        n_gt = lax.fori_loop(0, nkb, count_gt, jnp.zeros((8, tq), I32)).sum(axis=0, keepdims=True)
        need = (ksel - n_gt).astype(F32)

        def mask_block(kb, is_last, ties_before):
            kk = key_scr[kb]
            eq = kk == thr
            rank = jnp.dot(tri_ref[...], jnp.where(eq, 1.0, 0.0).astype(BF16),
                           preferred_element_type=F32) + ties_before
            tie = jnp.where(eq, jnp.where(rank <= need, 0.0, NEG_INF), NEG_INF)
            keep = jnp.where(kk > thr, 0.0, tie)
            mb_scr[kb] = (jnp.where(last_ok, keep, NEG_INF) if is_last else keep).astype(BF16)
            return rank[KEY_BLOCK - 1:KEY_BLOCK, :]

        over_blocks(mask_block, jnp.zeros((1, tq), F32))

    nsub = tq // BIAS_TILE

    def bias_block(h, diag):
        same = tb_ref[h, 1]
        prev = tb_ref[h, 0]
        far = jnp.broadcast_to(tb_ref[h, 0, 0:1, 0:1], (BIAS_TILE, BIAS_TILE))
        if diag:
            grid = [[same, prev], [far, same]]
        else:
            grid = [[far, far], [prev, far]]
        return jnp.concatenate([jnp.concatenate(r[:nsub], axis=1) for r in grid], axis=0)

    @pl.when(jnp.logical_and(pl.program_id(0) == 0, i == 0))
    def _():
        for h in range(N_HEADS):
            bias_scr[0, h] = bias_block(h, False).astype(BF16)
            bias_scr[1, h] = bias_block(h, True).astype(BF16)

    def produce(h, kb, kind, buf, gated=False):
        t_buf, s_buf = buf
        g, j = divmod(h, GROUP)
        lanes = slice(j * tq, (j + 1) * tq)
        t = jnp.dot(k_ref[0, g, key_rows(kb), :], qt_ref[0, g, 0, :, lanes],
                    preferred_element_type=F32).astype(BF16) + mb_scr[kb]
        if kind == "far":
            shift = jnp.broadcast_to(tb_ref[h, 0, 0:1, 0:1], (1, tq))
        else:
            shift = jnp.zeros((1, tq), F32)
            t = t + bias_scr[int(kind == "diag"), h]
            if gated:
                t = t + jnp.where(nkb >= 2, 0.0, NEG_INF).astype(BF16)
        t_buf[g, :, lanes] = t
        s_buf[g, 0:1, lanes] = jnp.max(t, axis=0, keepdims=True).astype(F32) + shift
        s_buf[g, 1:2, lanes] = shift

    def consume(h, kb, buf):
        t_buf, s_buf = buf
        g, j = divmod(h, GROUP)
        lanes = slice(j * tq, (j + 1) * tq)
        m = m_scr[g, :, lanes]
        shift = s_buf[g, 1:2, lanes]
        m_new = jnp.maximum(m, s_buf[g, 0:1, lanes])
        sub = (jnp.where(m_new == NEG_INF, 0.0, m_new) - shift).astype(BF16)
        m_used = sub.astype(F32) + shift
        p = jnp.exp2(t_buf[g, :, lanes] - sub)
        acc_scr[g, :, lanes] = jnp.exp2(m - m_used) * acc_scr[g, :, lanes] + jnp.dot(
            vt_ref[0, g, kb], p, preferred_element_type=F32)
        m_scr[g, :, lanes] = jnp.where(m_new == NEG_INF, NEG_INF, m_used)

    def stage(new=None, old=None):
        for h in range(N_HEADS):
            if new is not None:
                produce(h, *new)
            if old is not None:
                consume(h, *old)

    m_scr[...] = jnp.full(m_scr.shape, NEG_INF, F32)
    acc_scr[...] = jnp.zeros(acc_scr.shape, F32)

    buf_a, buf_b = (ta_scr, sa_scr), (tb_scr, sb_scr)
    n_far = jnp.maximum(nkb - 2, 0)
    off = jnp.maximum(nkb - 2, 0)
    stage(new=(last, "diag", buf_a))
    stage(new=(off, "off", buf_b, True), old=(last, buf_a))

    def far_pair(it, carry):
        f = nkb - 3 - 2 * it
        stage(new=(f, "far", buf_a), old=(f + 1, buf_b))
        stage(new=(f - 1, "far", buf_b), old=(f, buf_a))
        return carry

    lax.fori_loop(0, n_far // 2, far_pair, 0)

    @pl.when(n_far % 2 == 1)
    def _():
        stage(new=(0, "far", buf_a), old=(1, buf_b))
        stage(old=(0, buf_a))

    @pl.when(n_far % 2 == 0)
    def _():
        stage(old=(jnp.maximum(nkb - 2 - n_far, 0), buf_b))

    heads_t = []
    for g in range(N_KV_HEADS):
        acc = acc_scr[g]
        o = acc[:HEAD_DIM] / acc[HEAD_DIM:HEAD_DIM + 1]
        heads_t += [o[:, j * tq:(j + 1) * tq] for j in range(GROUP)]
    o_ref[0] = jnp.concatenate(heads_t, axis=0).T.astype(o_ref.dtype)


def _attention(qt, qit, wi_t, k_hm, vt, kib, tb, tri, *, tq, past, ksel):
    b, _, nq, _, _ = qt.shape
    lk = k_hm.shape[2]
    nkb_max = lk // KEY_BLOCK
    blocks = (nkb_max, KEY_BLOCK, tq)
    pairs = ((nkb_max + 1) // 2, 2 * KEY_BLOCK, tq)
    return pl.pallas_call(
        functools.partial(_attn_kernel, tq=tq, past=past, ksel=ksel),
        grid=(b, nq),
        in_specs=[
            pl.BlockSpec((1, N_KV_HEADS, 1, QK_DEPTH, GROUP * tq), lambda bi, i: (bi, 0, i, 0, 0)),
            pl.BlockSpec((1, 1, QK_DEPTH, IDX_HEADS * tq), lambda bi, i: (bi, i, 0, 0)),
            pl.BlockSpec((1, IDX_HEADS, tq), lambda bi, i: (bi, 0, i)),
            pl.BlockSpec((1, N_KV_HEADS, lk, QK_DEPTH), lambda bi, i: (bi, 0, 0, 0)),
            pl.BlockSpec((1, N_KV_HEADS, nkb_max, vt.shape[3], KEY_BLOCK),
                         lambda bi, i: (bi, 0, 0, 0, 0)),
            pl.BlockSpec((1, lk, QK_DEPTH), lambda bi, i: (bi, 0, 0)),
            pl.BlockSpec((N_HEADS, 2, BIAS_TILE, BIAS_TILE), lambda bi, i: (0, 0, 0, 0)),
            pl.BlockSpec((KEY_BLOCK, KEY_BLOCK), lambda bi, i: (0, 0)),
        ],
        out_specs=pl.BlockSpec((1, tq, N_HEADS * HEAD_DIM), lambda bi, i: (bi, i, 0)),
        out_shape=jax.ShapeDtypeStruct((b, nq * tq, N_HEADS * HEAD_DIM), BF16),
        scratch_shapes=[pltpu.VMEM(blocks, I32), pltpu.VMEM(pairs, I16), pltpu.VMEM(pairs, I16),
                        pltpu.VMEM(blocks, BF16),
                        pltpu.VMEM((N_KV_HEADS, 1, GROUP * tq), F32),
                        pltpu.VMEM((N_KV_HEADS, vt.shape[3], GROUP * tq), F32)]
        + 2 * [pltpu.VMEM((N_KV_HEADS, KEY_BLOCK, GROUP * tq), BF16),
               pltpu.VMEM((N_KV_HEADS, 2, GROUP * tq), F32)]
        + [pltpu.VMEM((2, N_HEADS, KEY_BLOCK, tq), BF16)],
        compiler_params=_params("arbitrary", "arbitrary"),
        name="attention",
    )(qt, qit, wi_t, k_hm, vt, kib, tb, tri)


def _post_kernel(x_ref, mod_ref, u_ref, halo_ref, prev_ref, sgc_ref, o_ref, sga_ref, gm_ref,
                 cw_ref, cb_ref, lg_ref, lb_ref, wc_ref, wa_ref, wo_ref, gp_ref,
                 y_ref, up_scr, sh_scr, cv_scr, *, tt, d_conv, d_model):
    i = pl.program_id(1)
    up_scr[0:CONV_HALO] = jnp.where(i == 0, prev_ref[0], halo_ref[0])
    up_scr[CONV_HALO:] = u_ref[0]
    span = tt + CONV_HALO - SUBLANES
    for s in range(1, SUBLANES):
        sh_scr[s - 1] = up_scr[pl.ds(s, span), :]
    pad = CONV_HALO - (CONV_W - 1)
    rt = min(tt, CONV_ROWS)
    for r0 in range(0, tt, rt):
        for c0 in range(0, d_conv, LANES):
            acc = jnp.broadcast_to(cb_ref[:, c0:c0 + LANES], (rt, LANES))
            for j in range(CONV_W):
                a, s = divmod(pad + j, SUBLANES)
                rows = pl.ds(r0 + SUBLANES * a, rt)
                win = up_scr[rows, c0:c0 + LANES] if s == 0 else sh_scr[s - 1, rows, c0:c0 + LANES]
                acc = acc + cw_ref[j:j + 1, c0:c0 + LANES] * win
            cv_scr[r0:r0 + rt, c0:c0 + LANES] = acc
    cv = cv_scr[...]
    mu = jnp.mean(cv, axis=-1, keepdims=True)
    dv = cv - mu
    var = jnp.mean(dv * dv, axis=-1, keepdims=True)
    yc = _silu(dv * lax.rsqrt(var + EPS) * lg_ref[...] + lb_ref[...])
    yc = (yc * sgc_ref[0].astype(F32)).astype(BF16)
    y_c = jnp.dot(yc, wc_ref[...], preferred_element_type=F32)
    ya = (o_ref[0].astype(F32) * sga_ref[0].astype(F32)).astype(BF16)
    y_a = jnp.dot(ya, wa_ref[...], preferred_element_type=F32)
    gm = gm_ref[0].astype(F32)
    mix = (gm[:, :d_model] * y_c + gm[:, d_model:] * y_a).astype(BF16)
    y = jnp.dot(mix, wo_ref[...], preferred_element_type=F32)
    y = y * lax.rsqrt(jnp.mean(y * y, axis=-1, keepdims=True) + EPS) * gp_ref[...]
    y_ref[0] = x_ref[0] + mod_ref[0, 2:3, :] * y


def _post(x, mod3, u, prev, sgc, o, sga, gm, conv_w, conv_b, ln_g, ln_b, wc, wa, wo, g_post):
    b, l, d = x.shape
    d_conv = u.shape[2]
    d_attn = o.shape[2]
    tt = min(POST_ROWS, l)
    hb = tt // CONV_HALO

    def row(width):
        return pl.BlockSpec((1, tt, width), lambda bi, i: (bi, i, 0))

    def const(shape):
        return pl.BlockSpec(shape, lambda bi, i: (0,) * len(shape))

    return pl.pallas_call(
        functools.partial(_post_kernel, tt=tt, d_conv=d_conv, d_model=d),
        grid=(b, l // tt),
        in_specs=[
            row(d),
            pl.BlockSpec((1, 3, d), lambda bi, i: (bi, 0, 0)),
            row(d_conv),
            pl.BlockSpec((1, CONV_HALO, d_conv), lambda bi, i: (bi, jnp.maximum(i * hb - 1, 0), 0)),
            pl.BlockSpec((1, CONV_HALO, d_conv), lambda bi, i: (bi, 0, 0)),
            row(d_conv), row(d_attn), row(d_attn), row(2 * d),
            const((CONV_W, d_conv)), const((1, d_conv)), const((1, d_conv)), const((1, d_conv)),
            const((d_conv, d)), const((d_attn, d)), const((d, d)), const((1, d)),
        ],
        out_specs=row(d),
        out_shape=jax.ShapeDtypeStruct((b, l, d), F32),
        scratch_shapes=[pltpu.VMEM((tt + CONV_HALO, d_conv), F32),
                        pltpu.VMEM((SUBLANES - 1, tt + CONV_HALO - SUBLANES, d_conv), F32),
                        pltpu.VMEM((tt, d_conv), F32)],
        compiler_params=_params("arbitrary", "arbitrary"),
        name="post",
    )(x, mod3, u, u, prev, sgc, o, sga, gm, conv_w, conv_b, ln_g, ln_b, wc, wa, wo, g_post)


def _pad_axis(t, axis, size):
    if t.shape[axis] == size:
        return t
    widths = [(0, 0)] * t.ndim
    widths[axis] = (0, size - t.shape[axis])
    return jnp.pad(t, widths)


def _sublayer(x, l_true, mod3, conv_prev, past, tb, tri, w):
    b, l, d = x.shape
    d_conv = w["conv_w"].shape[1]
    d_attn = N_HEADS * HEAD_DIM
    (u, sgc, qt, k, v, k_hm, vt, sga, qit, ki, kib, wi_t, gm) = _inproj(
        x, mod3, w["g_pre"], w["w_in"], d_conv=d_conv, d_attn=d_attn)

    if past is None:
        p_len = 0
    else:
        ck, cv, cki = past
        p_len = ck.shape[1]
        k_hm = jnp.concatenate(
            [_pad_axis(ck.transpose(0, 2, 1, 3).astype(BF16), 3, QK_DEPTH), k_hm], axis=2)
        ones_rows = jnp.zeros((b, N_KV_HEADS, PACKED_ROWS, p_len), BF16).at[:, :, 0].set(1.0)
        cvt = jnp.concatenate([cv.transpose(0, 2, 3, 1).astype(BF16), ones_rows], axis=2)
        cvt = cvt.reshape(b, N_KV_HEADS, cvt.shape[2], p_len // KEY_BLOCK, KEY_BLOCK)
        vt = jnp.concatenate([cvt.transpose(0, 1, 3, 2, 4), _pad_axis(vt, 4, KEY_BLOCK)], axis=2)
        kib = jnp.concatenate([_pad_axis(cki.astype(BF16), 2, QK_DEPTH), kib], axis=1)
    ksel = min(TOPK_MAX, (p_len + l_true) // 4)
    tq = min(KEY_BLOCK, l)
    lk_pad = -(-(p_len + l) // KEY_BLOCK) * KEY_BLOCK
    assert p_len % KEY_BLOCK == 0 and (tq == KEY_BLOCK or l == tq)
    assert (p_len + l_true) % CHUNK == 0 and ksel <= KEY_BLOCK
    o = _attention(qt, qit, wi_t, _pad_axis(k_hm, 2, lk_pad), vt, _pad_axis(kib, 1, lk_pad),
                   tb, tri, tq=tq, past=p_len, ksel=ksel)

    prev = jnp.pad(conv_prev, ((0, 0), (CONV_HALO - (CONV_W - 1), 0), (0, 0)))
    y = _post(x, mod3, u, prev, sgc, o, sga, gm, w["conv_w"], w["conv_b"], w["ln_g"], w["ln_b"],
              w["w_conv_out"], w["w_attn_out"], w["w_out"], w["g_post"])
    conv_new = jnp.concatenate([conv_prev, u[:, :l_true]], axis=1)[:, -(CONV_W - 1):]
    return (y, conv_new, k[:, :l_true].reshape(b, l_true, N_KV_HEADS, HEAD_DIM),
            v[:, :l_true].reshape(b, l_true, N_KV_HEADS, HEAD_DIM), ki[:, :l_true])


def _pack_w_in(w_in, d_conv, d_attn, d_model):
    kvd = N_KV_HEADS * HEAD_DIM
    splits = (2 * d_conv, d_conv, d_attn, kvd, kvd, d_attn, IDX_HEADS * IDX_DIM, IDX_DIM,
              IDX_HEADS, 2 * d_model)
    offs = [0]
    for s in splits:
        offs.append(offs[-1] + s)
    seg = [w_in[:, offs[j]:offs[j + 1]] for j in range(len(splits))]
    kw = jnp.concatenate([seg[7], seg[8]], axis=1)
    kw = jnp.pad(kw, ((0, 0), (0, LANES - kw.shape[1])))
    return jnp.concatenate(seg[:7] + [kw, seg[9]], axis=1).astype(BF16)


def kernel(x_prompt, x_sample, c_prompt, c_sample, cache_k, cache_v, cache_kidx, state_conv,
           rel_bias, ada_w, ada_b, norm_pre, norm_post, w_in, conv_w, conv_b, conv_ln_g,
           conv_ln_b, w_conv_out, w_attn_out, w_out):
    depth = ada_w.shape[0]
    bp, _, d = x_prompt.shape
    d_conv = conv_w.shape[2]
    d_attn = N_HEADS * HEAD_DIM
    mod = _ada(jnp.concatenate([c_prompt, c_sample], axis=0), ada_w, ada_b)
    mod = mod.reshape(depth, mod.shape[1], 3, d)
    tb = _bias_tiles(rel_bias)
    tri = (jnp.arange(KEY_BLOCK)[:, None] >= jnp.arange(KEY_BLOCK)[None, :]).astype(BF16)

    lp, ls = x_prompt.shape[1], x_sample.shape[1]
    xp, xs = x_prompt, _pad_axis(x_sample, 1, -(-ls // LANES) * LANES)
    outs = [[] for _ in range(8)]
    for l in range(depth):
        w = dict(
            g_pre=norm_pre[l][None], g_post=norm_post[l][None],
            w_in=_pack_w_in(w_in[l], d_conv, d_attn, d),
            conv_w=conv_w[l], conv_b=conv_b[l][None], ln_g=conv_ln_g[l][None],
            ln_b=conv_ln_b[l][None], w_conv_out=w_conv_out[l].astype(BF16),
            w_attn_out=w_attn_out[l].astype(BF16), w_out=w_out[l].astype(BF16))
        zeros_conv = jnp.zeros((bp, CONV_W - 1, d_conv), xp.dtype)
        xp, cp, kp, vp, kip = _sublayer(xp, lp, mod[l, :bp], zeros_conv, None, tb, tri, w)
        xs, cs, kss, vss, kis = _sublayer(xs, ls, mod[l, bp:], state_conv[l],
                                          (cache_k[l], cache_v[l], cache_kidx[l]), tb, tri, w)
        for lst, val in zip(outs, (kp, vp, kip, cp, kss, vss, kis, cs)):
            lst.append(val)
    return (xp, xs[:, :ls]) + tuple(jnp.stack(o) for o in outs)
```

```python
import functools
import math

import jax
import jax.numpy as jnp
from jax import lax
from jax.experimental import pallas as pl
from jax.experimental.pallas import tpu as pltpu

F32 = jnp.float32
BF16 = jnp.bfloat16
I32 = jnp.int32
I16 = jnp.int16

CHUNK = 64
CONV_W = 31
N_HEADS = 16
N_KV_HEADS = 4
HEAD_DIM = 64
GROUP = N_HEADS // N_KV_HEADS
IDX_HEADS = 8
IDX_DIM = 64
TOPK_MAX = 256
N_BUCKETS = 32
MAX_DIST = 128
EPS = 1e-6

LANES = 128
SUBLANES = 8
PACKED_ROWS = 16
KEY_BLOCK = 256
BIAS_TILE = 128
CONV_HALO = 32
CONV_ROWS = 128
POST_ROWS = 256
VMEM_LIMIT_BYTES = 56 * 1024 * 1024
NEG_INF = float("-inf")
INT_MIN = -2 ** 31
HALF_MIN = -2 ** 15
LOG2E = 1.4426950408889634


def _sigmoid(x):
    return 1.0 / (1.0 + jnp.exp(-x))


def _silu(x):
    return x * _sigmoid(x)


def _params(*sem):
    return pltpu.CompilerParams(dimension_semantics=sem, vmem_limit_bytes=VMEM_LIMIT_BYTES)


def _ada_kernel(c_ref, w_ref, b_ref, o_ref):
    c = c_ref[...]
    o_ref[0] = jnp.dot(_silu(c), w_ref[0], preferred_element_type=F32) + b_ref[0]


def _ada(c_all, ada_w, ada_b):
    depth, d, _ = ada_w.shape
    bc = c_all.shape[0]
    return pl.pallas_call(
        _ada_kernel,
        grid=(depth, 3),
        in_specs=[
            pl.BlockSpec((bc, d), lambda l, n: (0, 0)),
            pl.BlockSpec((1, d, d), lambda l, n: (l, 0, n)),
            pl.BlockSpec((1, 1, d), lambda l, n: (l, 0, n)),
        ],
        out_specs=pl.BlockSpec((1, bc, d), lambda l, n: (l, 0, n)),
        out_shape=jax.ShapeDtypeStruct((depth, bc, 3 * d), F32),
        compiler_params=_params("arbitrary", "arbitrary"),
        name="ada",
    )(c_all, ada_w, ada_b.reshape(depth, 1, 3 * d))


def _bias_kernel(rb_ref, o_ref):
    r = lax.broadcasted_iota(I32, (BIAS_TILE, BIAS_TILE), 0)
    c = lax.broadcasted_iota(I32, (BIAS_TILE, BIAS_TILE), 1)
    nb = N_BUCKETS // 2
    max_exact = nb // 2
    for t in range(2):
        rel = (t - 1) * BIAS_TILE + r - c
        ret = jnp.where(rel > 0, nb, 0)
        n = jnp.abs(rel)
        nf = jnp.maximum(n, 1).astype(F32)
        large = max_exact + (jnp.log(nf / max_exact) / math.log(MAX_DIST / max_exact)
                             * (nb - max_exact)).astype(I32)
        large = jnp.minimum(large, nb - 1)
        bucket = ret + jnp.where(n < max_exact, n, large)
        for h in range(N_HEADS):
            acc = jnp.zeros((BIAS_TILE, BIAS_TILE), F32)
            for b in range(N_BUCKETS):
                acc = jnp.where(bucket == b, rb_ref[b, h], acc)
            o_ref[h, t] = acc * LOG2E


def _bias_tiles(rel_bias):
    return pl.pallas_call(
        _bias_kernel,
        in_specs=[pl.BlockSpec(memory_space=pltpu.SMEM)],
        out_specs=pl.BlockSpec(memory_space=pltpu.VMEM),
        out_shape=jax.ShapeDtypeStruct((N_HEADS, 2, BIAS_TILE, BIAS_TILE), F32),
        name="bias_tiles",
    )(rel_bias)


def _seg_offsets(d_conv, d_attn):
    widths = (2 * d_conv, d_conv, d_attn, N_KV_HEADS * HEAD_DIM, N_KV_HEADS * HEAD_DIM, d_attn,
              IDX_HEADS * IDX_DIM, LANES, None)
    offs, acc = [], 0
    for w in widths:
        offs.append(acc)
        if w is not None:
            acc += w
    return offs


def _inproj_kernel(x_ref, mod_ref, g_ref, w_ref, _k_all, _v_all, _ki_all,
                   u_ref, sgc_ref, qt_ref, k_ref, v_ref, kb_ref, vt_ref, sga_ref, qit_ref,
                   ki_ref, kib_ref, wi_ref, gm_ref, *, tm, d_conv, d_attn, d_model):
    x = x_ref[0]
    ms = jnp.mean(x * x, axis=-1, keepdims=True)
    y = x * lax.rsqrt(ms + EPS) * g_ref[...]
    shift = mod_ref[0, 0:1, :]
    scale = mod_ref[0, 1:2, :]
    h = (y * (1.0 + scale) + shift).astype(BF16)
    o_a, o_gc, o_q, o_k, o_v, o_ga, o_qi, o_kw, o_gm = _seg_offsets(d_conv, d_attn)

    def mm(lo, width):
        return jnp.dot(h, w_ref[:, lo:lo + width], preferred_element_type=F32)

    a = mm(o_a, 2 * d_conv)
    u_ref[0] = a[:, :d_conv] * _sigmoid(a[:, d_conv:])
    sgc_ref[0] = _silu(mm(o_gc, d_conv)).astype(BF16)
    zqt = (mm(o_q, d_attn) * (HEAD_DIM ** -0.5 * LOG2E)).T
    for hh in range(N_HEADS):
        g, j = divmod(hh, GROUP)
        qt_ref[0, g, 0, :, j * tm:(j + 1) * tm] = (
            zqt[hh * HEAD_DIM:(hh + 1) * HEAD_DIM].astype(BF16))
    zk = mm(o_k, N_KV_HEADS * HEAD_DIM)
    zv = mm(o_v, N_KV_HEADS * HEAD_DIM)
    k_ref[0, 0] = zk
    v_ref[0, 0] = zv
    zvt = zv.T
    ones_rows = jnp.where(lax.broadcasted_iota(I32, (PACKED_ROWS, tm), 0) == 0, 1.0, 0.0)
    for g in range(N_KV_HEADS):
        kb_ref[0, g] = zk[:, g * HEAD_DIM:(g + 1) * HEAD_DIM].astype(BF16)
        vt_ref[0, g, 0, 0:HEAD_DIM] = zvt[g * HEAD_DIM:(g + 1) * HEAD_DIM].astype(BF16)
        vt_ref[0, g, 0, HEAD_DIM:] = ones_rows.astype(BF16)
    sga_ref[0] = _silu(mm(o_ga, d_attn)).astype(BF16)
    zqit = (mm(o_qi, IDX_HEADS * IDX_DIM) * (IDX_DIM ** -0.5)).T
    for hh in range(IDX_HEADS):
        qit_ref[0, 0, :, hh * tm:(hh + 1) * tm] = zqit[hh * IDX_DIM:(hh + 1) * IDX_DIM].astype(BF16)
    zkw = mm(o_kw, LANES)
    zki = zkw[:, :IDX_DIM]
    ki_ref[0, 0] = zki
    kib_ref[0] = zki.astype(BF16)
    wi_ref[0] = zkw.T[IDX_DIM:IDX_DIM + IDX_HEADS] * (IDX_HEADS ** -0.5)
    gm_ref[0] = _sigmoid(mm(o_gm, 2 * d_model)).astype(BF16)


def _inproj(x, mod3, g_pre, w_pad, states, layer, *, d_conv, d_attn):
    b, l, d = x.shape
    tm = min(KEY_BLOCK, l)
    nw = w_pad.shape[1]
    kvd = N_KV_HEADS * HEAD_DIM

    def row(width):
        return pl.BlockSpec((1, tm, width), lambda bi, i: (bi, i, 0))

    def heads(n, width):
        return pl.BlockSpec((1, n, tm, width), lambda bi, i: (bi, 0, i, 0))

    def layer_rows(width):
        return pl.BlockSpec((1, 1, tm, width), lambda bi, i: (layer, bi, i, 0))

    nq = l // tm
    vrows = HEAD_DIM + PACKED_ROWS
    k_all, v_all, ki_all = states
    out_shape = (
        jax.ShapeDtypeStruct((b, l, d_conv), F32),
        jax.ShapeDtypeStruct((b, l, d_conv), BF16),
        jax.ShapeDtypeStruct((b, N_KV_HEADS, nq, HEAD_DIM, GROUP * tm), BF16),
        jax.ShapeDtypeStruct(k_all.shape, F32),
        jax.ShapeDtypeStruct(v_all.shape, F32),
        jax.ShapeDtypeStruct((b, N_KV_HEADS, l, HEAD_DIM), BF16),
        jax.ShapeDtypeStruct((b, N_KV_HEADS, nq, vrows, tm), BF16),
        jax.ShapeDtypeStruct((b, l, d_attn), BF16),
        jax.ShapeDtypeStruct((b, nq, IDX_DIM, IDX_HEADS * tm), BF16),
        jax.ShapeDtypeStruct(ki_all.shape, F32),
        jax.ShapeDtypeStruct((b, l, IDX_DIM), BF16),
        jax.ShapeDtypeStruct((b, IDX_HEADS, l), F32),
        jax.ShapeDtypeStruct((b, l, 2 * d), BF16),
    )
    out_specs = (row(d_conv), row(d_conv),
                 pl.BlockSpec((1, N_KV_HEADS, 1, HEAD_DIM, GROUP * tm), lambda bi, i: (bi, 0, i, 0, 0)),
                 layer_rows(kvd), layer_rows(kvd), heads(N_KV_HEADS, HEAD_DIM),
                 pl.BlockSpec((1, N_KV_HEADS, 1, vrows, tm), lambda bi, i: (bi, 0, i, 0, 0)),
                 row(d_attn),
                 pl.BlockSpec((1, 1, IDX_DIM, IDX_HEADS * tm), lambda bi, i: (bi, i, 0, 0)),
                 layer_rows(IDX_DIM), row(IDX_DIM),
                 pl.BlockSpec((1, IDX_HEADS, tm), lambda bi, i: (bi, 0, i)),
                 row(2 * d))
    return pl.pallas_call(
        functools.partial(_inproj_kernel, tm=tm, d_conv=d_conv, d_attn=d_attn, d_model=d),
        grid=(b, l // tm),
        in_specs=[
            pl.BlockSpec((1, tm, d), lambda bi, i: (bi, i, 0)),
            pl.BlockSpec((1, 3, d), lambda bi, i: (bi, 0, 0)),
            pl.BlockSpec((1, d), lambda bi, i: (0, 0)),
            pl.BlockSpec((d, nw), lambda bi, i: (0, 0), pipeline_mode=pl.Buffered(1)),
            pl.BlockSpec(memory_space=pl.ANY),
            pl.BlockSpec(memory_space=pl.ANY),
            pl.BlockSpec(memory_space=pl.ANY),
        ],
        out_specs=out_specs,
        out_shape=out_shape,
        input_output_aliases={4: 3, 5: 4, 6: 9},
        compiler_params=_params("arbitrary", "arbitrary"),
        name="inproj",
    )(x, mod3, g_pre, w_pad, k_all, v_all, ki_all)


def _attn_kernel(qt_ref, qit_ref, wi_ref, k_ref, vt_ref, ki_ref, tb_ref, tri_ref, o_ref,
                 key_scr, hi_scr, lo_scr, mb_scr, m_scr, acc_scr, ta_scr, sa_scr, tb_scr, sb_scr,
                 bias_scr, *, tq, past, ksel):
    i = pl.program_id(1)
    q0 = past + i * tq
    nkb = (q0 + tq + KEY_BLOCK - 1) // KEY_BLOCK

    def key_rows(kb):
        return pl.ds(pl.multiple_of(kb * KEY_BLOCK, KEY_BLOCK), KEY_BLOCK)

    qit = qit_ref[0, 0]
    wi = wi_ref[0]
    last = nkb - 1
    last_ok = (lax.broadcasted_iota(I32, (KEY_BLOCK, tq), 0) // CHUNK
               <= lax.broadcasted_iota(I32, (KEY_BLOCK, tq), 1) // CHUNK)

    def score_block(kb, is_last):
        ki_blk = ki_ref[0, key_rows(kb), :]
        acc = jnp.zeros((KEY_BLOCK, tq), F32)
        for h in range(IDX_HEADS):
            s = jnp.dot(ki_blk, qit[:, h * tq:(h + 1) * tq], preferred_element_type=F32)
            acc = acc + jnp.maximum(s, 0.0) * wi[h:h + 1, :]
        sc = jnp.where(last_ok, acc, NEG_INF) if is_last else acc
        bits = pltpu.bitcast(sc, I32)
        key = bits ^ ((bits >> 31) & 0x7FFFFFFF)
        key_scr[kb] = key
        half = pl.ds(pl.multiple_of((kb % 2) * KEY_BLOCK, KEY_BLOCK), KEY_BLOCK)
        hi_scr[kb // 2, half, :] = (key >> 16).astype(I16)
        lo_scr[kb // 2, half, :] = ((key & 0xFFFF) + HALF_MIN).astype(I16)

    def over_blocks(fn, init=0):
        carry = lax.fori_loop(0, last, lambda kb, c: fn(kb, False, c), init)
        return fn(last, True, carry)

    def score_pair(kp, carry):
        score_block(2 * kp, False)
        score_block(2 * kp + 1, False)
        return carry

    lax.fori_loop(0, last // 2, score_pair, 0)

    @pl.when(last % 2 == 1)
    def _():
        score_block(last - 1, False)

    score_block(last, True)

    npair = (nkb + 1) // 2

    @pl.when(nkb % 2 == 1)
    def _():
        floor16 = jnp.full((KEY_BLOCK, tq), HALF_MIN, I16)
        hi_scr[npair - 1, KEY_BLOCK:, :] = floor16
        lo_scr[npair - 1, KEY_BLOCK:, :] = floor16

    def count16(scr, pred, visit=None):
        groups = 2 * KEY_BLOCK // PACKED_ROWS

        def body(kp, acc):
            x = scr[kp]
            if visit is not None:
                visit(kp, x)
            hit = jnp.where(pred(x), jnp.int16(1), jnp.int16(0))
            hit = hit.reshape(groups, PACKED_ROWS, tq)
            parts = [hit[r] for r in range(groups)]
            while len(parts) > 1:
                parts = [parts[r] + parts[r + 1] for r in range(0, len(parts), 2)]
            return acc + parts[0]
        acc = lax.fori_loop(0, npair, body, jnp.zeros((PACKED_ROWS, tq), I16))
        return acc.astype(I32).sum(axis=0, keepdims=True)

    def search16(scr, target):
        def refine(it, state):
            thr, n_at = state
            cand = thr + jnp.left_shift(jnp.int32(1), 15 - it)
            c16 = cand.astype(I16)
            n = count16(scr, lambda x: x >= c16)
            return jnp.where(n >= target, cand, thr), jnp.where(n >= target, n, n_at)
        every = jnp.full((1, tq), 2 * KEY_BLOCK, I32) * npair
        return lax.fori_loop(0, 16, refine, (jnp.full((1, tq), HALF_MIN, I32), every))

    thr_hi, _ = search16(hi_scr, ksel)
    thr_hi16 = thr_hi.astype(I16)

    def restrict(kp, hi):
        lo_scr[kp] = jnp.where(hi == thr_hi16, lo_scr[kp], jnp.int16(HALF_MIN))

    above = count16(hi_scr, lambda x: x > thr_hi16, visit=restrict)
    thr_lo, n_lo = search16(lo_scr, ksel - above)
    thr = thr_hi * 65536 + (thr_lo - HALF_MIN)
    surplus = jnp.max(above + n_lo) > ksel

    @pl.when(jnp.logical_not(surplus))
    def _():
        def mask_block(kb, is_last, carry):
            keep = jnp.where(key_scr[kb] >= thr, 0.0, NEG_INF)
            mb_scr[kb] = (jnp.where(last_ok, keep, NEG_INF) if is_last else keep).astype(BF16)
            return carry

        over_blocks(mask_block)

    @pl.when(surplus)
    def _():
        def count_gt(kb, acc):
            hit = (key_scr[kb] > thr).astype(I32)
            return acc + hit.reshape(KEY_BLOCK // 8, 8, tq).sum(axis=0)

        n_gt = lax.fori_loop(0, nkb, count_gt, jnp.zeros((8, tq), I32)).sum(axis=0, keepdims=True)
        need = (ksel - n_gt).astype(F32)

        def mask_block(kb, is_last, ties_before):
            kk = key_scr[kb]
            eq = kk == thr
            rank = jnp.dot(tri_ref[...], jnp.where(eq, 1.0, 0.0).astype(BF16),
                           preferred_element_type=F32) + ties_before
            tie = jnp.where(eq, jnp.where(rank <= need, 0.0, NEG_INF), NEG_INF)
            keep = jnp.where(kk > thr, 0.0, tie)
            mb_scr[kb] = (jnp.where(last_ok, keep, NEG_INF) if is_last else keep).astype(BF16)
            return rank[KEY_BLOCK - 1:KEY_BLOCK, :]

        over_blocks(mask_block, jnp.zeros((1, tq), F32))

    nsub = tq // BIAS_TILE

    def bias_block(h, diag):
        same = tb_ref[h, 1]
        prev = tb_ref[h, 0]
        far = jnp.broadcast_to(tb_ref[h, 0, 0:1, 0:1], (BIAS_TILE, BIAS_TILE))
        if diag:
            grid = [[same, prev], [far, same]]
        else:
            grid = [[far, far], [prev, far]]
        return jnp.concatenate([jnp.concatenate(r[:nsub], axis=1) for r in grid], axis=0)

    @pl.when(jnp.logical_and(pl.program_id(0) == 0, i == 0))
    def _():
        for h in range(N_HEADS):
            bias_scr[0, h] = bias_block(h, False).astype(BF16)
            bias_scr[1, h] = bias_block(h, True).astype(BF16)

    def produce(h, kb, kind, buf, gated=False):
        t_buf, s_buf = buf
        g, j = divmod(h, GROUP)
        lanes = slice(j * tq, (j + 1) * tq)
        t = jnp.dot(k_ref[0, g, key_rows(kb), :], qt_ref[0, g, 0, :, lanes],
                    preferred_element_type=F32).astype(BF16) + mb_scr[kb]
        if kind == "far":
            shift = jnp.broadcast_to(tb_ref[h, 0, 0:1, 0:1], (1, tq))
        else:
            shift = jnp.zeros((1, tq), F32)
            t = t + bias_scr[int(kind == "diag"), h]
            if gated:
                t = t + jnp.where(nkb >= 2, 0.0, NEG_INF).astype(BF16)
        t_buf[g, :, lanes] = t
        s_buf[g, 0:1, lanes] = jnp.max(t, axis=0, keepdims=True).astype(F32) + shift
        s_buf[g, 1:2, lanes] = shift

    def consume(h, kb, buf):
        t_buf, s_buf = buf
        g, j = divmod(h, GROUP)
        lanes = slice(j * tq, (j + 1) * tq)
        m = m_scr[g, :, lanes]
        shift = s_buf[g, 1:2, lanes]
        m_new = jnp.maximum(m, s_buf[g, 0:1, lanes])
        sub = (jnp.where(m_new == NEG_INF, 0.0, m_new) - shift).astype(BF16)
        m_used = sub.astype(F32) + shift
        p = jnp.exp2(t_buf[g, :, lanes] - sub)
        acc_scr[g, :, lanes] = jnp.exp2(m - m_used) * acc_scr[g, :, lanes] + jnp.dot(
            vt_ref[0, g, kb], p, preferred_element_type=F32)
        m_scr[g, :, lanes] = jnp.where(m_new == NEG_INF, NEG_INF, m_used)

    def stage(new=None, old=None):
        for h in range(N_HEADS):
            if new is not None:
                produce(h, *new)
            if old is not None:
                consume(h, *old)

    m_scr[...] = jnp.full(m_scr.shape, NEG_INF, F32)
    acc_scr[...] = jnp.zeros(acc_scr.shape, F32)

    buf_a, buf_b = (ta_scr, sa_scr), (tb_scr, sb_scr)
    n_far = jnp.maximum(nkb - 2, 0)
    off = jnp.maximum(nkb - 2, 0)
    stage(new=(last, "diag", buf_a))
    stage(new=(off, "off", buf_b, True), old=(last, buf_a))

    def far_pair(it, carry):
        f = nkb - 3 - 2 * it
        stage(new=(f, "far", buf_a), old=(f + 1, buf_b))
        stage(new=(f - 1, "far", buf_b), old=(f, buf_a))
        return carry

    lax.fori_loop(0, n_far // 2, far_pair, 0)

    @pl.when(n_far % 2 == 1)
    def _():
        stage(new=(0, "far", buf_a), old=(1, buf_b))
        stage(old=(0, buf_a))

    @pl.when(n_far % 2 == 0)
    def _():
        stage(old=(jnp.maximum(nkb - 2 - n_far, 0), buf_b))

    heads_t = []
    for g in range(N_KV_HEADS):
        acc = acc_scr[g]
        o = acc[:HEAD_DIM] / acc[HEAD_DIM:HEAD_DIM + 1]
        heads_t += [o[:, j * tq:(j + 1) * tq] for j in range(GROUP)]
    o_ref[0] = jnp.concatenate(heads_t, axis=0).T.astype(o_ref.dtype)


def _attention(qt, qit, wi_t, k_hm, vt, kib, tb, tri, *, tq, past, ksel):
    b, _, nq, _, _ = qt.shape
    lk = k_hm.shape[2]
    nkb_max = lk // KEY_BLOCK
    blocks = (nkb_max, KEY_BLOCK, tq)
    pairs = ((nkb_max + 1) // 2, 2 * KEY_BLOCK, tq)
    return pl.pallas_call(
        functools.partial(_attn_kernel, tq=tq, past=past, ksel=ksel),
        grid=(b, nq),
        in_specs=[
            pl.BlockSpec((1, N_KV_HEADS, 1, HEAD_DIM, GROUP * tq), lambda bi, i: (bi, 0, i, 0, 0)),
            pl.BlockSpec((1, 1, IDX_DIM, IDX_HEADS * tq), lambda bi, i: (bi, i, 0, 0)),
            pl.BlockSpec((1, IDX_HEADS, tq), lambda bi, i: (bi, 0, i)),
            pl.BlockSpec((1, N_KV_HEADS, lk, HEAD_DIM), lambda bi, i: (bi, 0, 0, 0)),
            pl.BlockSpec((1, N_KV_HEADS, nkb_max, vt.shape[3], KEY_BLOCK),
                         lambda bi, i: (bi, 0, 0, 0, 0)),
            pl.BlockSpec((1, lk, IDX_DIM), lambda bi, i: (bi, 0, 0)),
            pl.BlockSpec((N_HEADS, 2, BIAS_TILE, BIAS_TILE), lambda bi, i: (0, 0, 0, 0)),
            pl.BlockSpec((KEY_BLOCK, KEY_BLOCK), lambda bi, i: (0, 0)),
        ],
        out_specs=pl.BlockSpec((1, tq, N_HEADS * HEAD_DIM), lambda bi, i: (bi, i, 0)),
        out_shape=jax.ShapeDtypeStruct((b, nq * tq, N_HEADS * HEAD_DIM), BF16),
        scratch_shapes=[pltpu.VMEM(blocks, I32), pltpu.VMEM(pairs, I16), pltpu.VMEM(pairs, I16),
                        pltpu.VMEM(blocks, BF16),
                        pltpu.VMEM((N_KV_HEADS, 1, GROUP * tq), F32),
                        pltpu.VMEM((N_KV_HEADS, vt.shape[3], GROUP * tq), F32)]
        + 2 * [pltpu.VMEM((N_KV_HEADS, KEY_BLOCK, GROUP * tq), BF16),
               pltpu.VMEM((N_KV_HEADS, 2, GROUP * tq), F32)]
        + [pltpu.VMEM((2, N_HEADS, KEY_BLOCK, tq), BF16)],
        compiler_params=_params("arbitrary", "arbitrary"),
        name="attention",
    )(qt, qit, wi_t, k_hm, vt, kib, tb, tri)


def _post_kernel(x_ref, mod_ref, u_ref, halo_ref, prev_ref, sgc_ref, o_ref, sga_ref, gm_ref,
                 cw_ref, cb_ref, lg_ref, lb_ref, wc_ref, wa_ref, wo_ref, gp_ref,
                 y_ref, up_scr, sh_scr, cv_scr, *, tt, d_conv, d_model):
    i = pl.program_id(1)
    up_scr[0:CONV_HALO] = jnp.where(i == 0, prev_ref[0], halo_ref[0])
    up_scr[CONV_HALO:] = u_ref[0]
    span = tt + CONV_HALO - SUBLANES
    for s in range(1, SUBLANES):
        sh_scr[s - 1] = up_scr[pl.ds(s, span), :]
    pad = CONV_HALO - (CONV_W - 1)
    rt = min(tt, CONV_ROWS)
    for r0 in range(0, tt, rt):
        for c0 in range(0, d_conv, LANES):
            acc = jnp.broadcast_to(cb_ref[:, c0:c0 + LANES], (rt, LANES))
            for j in range(CONV_W):
                a, s = divmod(pad + j, SUBLANES)
                rows = pl.ds(r0 + SUBLANES * a, rt)
                win = up_scr[rows, c0:c0 + LANES] if s == 0 else sh_scr[s - 1, rows, c0:c0 + LANES]
                acc = acc + cw_ref[j:j + 1, c0:c0 + LANES] * win
            cv_scr[r0:r0 + rt, c0:c0 + LANES] = acc
    cv = cv_scr[...]
    mu = jnp.mean(cv, axis=-1, keepdims=True)
    dv = cv - mu
    var = jnp.mean(dv * dv, axis=-1, keepdims=True)
    yc = _silu(dv * lax.rsqrt(var + EPS) * lg_ref[...] + lb_ref[...])
    yc = (yc * sgc_ref[0].astype(F32)).astype(BF16)
    y_c = jnp.dot(yc, wc_ref[...], preferred_element_type=F32)
    ya = (o_ref[0].astype(F32) * sga_ref[0].astype(F32)).astype(BF16)
    y_a = jnp.dot(ya, wa_ref[...], preferred_element_type=F32)
    gm = gm_ref[0].astype(F32)
    mix = (gm[:, :d_model] * y_c + gm[:, d_model:] * y_a).astype(BF16)
    y = jnp.dot(mix, wo_ref[...], preferred_element_type=F32)
    y = y * lax.rsqrt(jnp.mean(y * y, axis=-1, keepdims=True) + EPS) * gp_ref[...]
    y_ref[0] = x_ref[0] + mod_ref[0, 2:3, :] * y


def _post(x, mod3, u, prev, sgc, o, sga, gm, conv_w, conv_b, ln_g, ln_b, wc, wa, wo, g_post):
    b, l, d = x.shape
    d_conv = u.shape[2]
    d_attn = o.shape[2]
    tt = min(POST_ROWS, l)
    hb = tt // CONV_HALO

    def row(width):
        return pl.BlockSpec((1, tt, width), lambda bi, i: (bi, i, 0))

    def const(shape):
        return pl.BlockSpec(shape, lambda bi, i: (0,) * len(shape))

    return pl.pallas_call(
        functools.partial(_post_kernel, tt=tt, d_conv=d_conv, d_model=d),
        grid=(b, l // tt),
        in_specs=[
            row(d),
            pl.BlockSpec((1, 3, d), lambda bi, i: (bi, 0, 0)),
            row(d_conv),
            pl.BlockSpec((1, CONV_HALO, d_conv), lambda bi, i: (bi, jnp.maximum(i * hb - 1, 0), 0)),
            pl.BlockSpec((1, CONV_HALO, d_conv), lambda bi, i: (bi, 0, 0)),
            row(d_conv), row(d_attn), row(d_attn), row(2 * d),
            const((CONV_W, d_conv)), const((1, d_conv)), const((1, d_conv)), const((1, d_conv)),
            const((d_conv, d)), const((d_attn, d)), const((d, d)), const((1, d)),
        ],
        out_specs=row(d),
        out_shape=jax.ShapeDtypeStruct((b, l, d), F32),
        scratch_shapes=[pltpu.VMEM((tt + CONV_HALO, d_conv), F32),
                        pltpu.VMEM((SUBLANES - 1, tt + CONV_HALO - SUBLANES, d_conv), F32),
                        pltpu.VMEM((tt, d_conv), F32)],
        compiler_params=_params("arbitrary", "arbitrary"),
        name="post",
    )(x, mod3, u, u, prev, sgc, o, sga, gm, conv_w, conv_b, ln_g, ln_b, wc, wa, wo, g_post)


def _pad_axis(t, axis, size):
    if t.shape[axis] == size:
        return t
    widths = [(0, 0)] * t.ndim
    widths[axis] = (0, size - t.shape[axis])
    return jnp.pad(t, widths)


def _sublayer(x, l_true, mod3, conv_prev, past, tb, tri, w, states, layer):
    b, l, d = x.shape
    d_conv = w["conv_w"].shape[1]
    d_attn = N_HEADS * HEAD_DIM
    (u, sgc, qt, k_all, v_all, k_hm, vt, sga, qit, ki_all, kib, wi_t, gm) = _inproj(
        x, mod3, w["g_pre"], w["w_in"], states, layer, d_conv=d_conv, d_attn=d_attn)

    if past is None:
        p_len = 0
    else:
        ck, cv, cki = past
        p_len = ck.shape[1]
        k_hm = jnp.concatenate([ck.transpose(0, 2, 1, 3).astype(BF16), k_hm], axis=2)
        ones_rows = jnp.zeros((b, N_KV_HEADS, PACKED_ROWS, p_len), BF16).at[:, :, 0].set(1.0)
        cvt = jnp.concatenate([cv.transpose(0, 2, 3, 1).astype(BF16), ones_rows], axis=2)
        cvt = cvt.reshape(b, N_KV_HEADS, cvt.shape[2], p_len // KEY_BLOCK, KEY_BLOCK)
        vt = jnp.concatenate([cvt.transpose(0, 1, 3, 2, 4), _pad_axis(vt, 4, KEY_BLOCK)], axis=2)
        kib = jnp.concatenate([cki.astype(BF16), kib], axis=1)
    ksel = min(TOPK_MAX, (p_len + l_true) // 4)
    tq = min(KEY_BLOCK, l)
    lk_pad = -(-(p_len + l) // KEY_BLOCK) * KEY_BLOCK
    assert p_len % KEY_BLOCK == 0 and (tq == KEY_BLOCK or l == tq)
    assert (p_len + l_true) % CHUNK == 0 and ksel <= KEY_BLOCK
    o = _attention(qt, qit, wi_t, _pad_axis(k_hm, 2, lk_pad), vt, _pad_axis(kib, 1, lk_pad),
                   tb, tri, tq=tq, past=p_len, ksel=ksel)

    prev = jnp.pad(conv_prev, ((0, 0), (CONV_HALO - (CONV_W - 1), 0), (0, 0)))
    y = _post(x, mod3, u, prev, sgc, o, sga, gm, w["conv_w"], w["conv_b"], w["ln_g"], w["ln_b"],
              w["w_conv_out"], w["w_attn_out"], w["w_out"], w["g_post"])
    conv_new = jnp.concatenate([conv_prev, u[:, :l_true]], axis=1)[:, -(CONV_W - 1):]
    return y, conv_new, (k_all, v_all, ki_all)


def _pack_w_in(w_in, d_conv, d_attn, d_model):
    kvd = N_KV_HEADS * HEAD_DIM
    splits = (2 * d_conv, d_conv, d_attn, kvd, kvd, d_attn, IDX_HEADS * IDX_DIM, IDX_DIM,
              IDX_HEADS, 2 * d_model)
    offs = [0]
    for s in splits:
        offs.append(offs[-1] + s)
    seg = [w_in[:, offs[j]:offs[j + 1]] for j in range(len(splits))]
    kw = jnp.concatenate([seg[7], seg[8]], axis=1)
    kw = jnp.pad(kw, ((0, 0), (0, LANES - kw.shape[1])))
    return jnp.concatenate(seg[:7] + [kw, seg[9]], axis=1).astype(BF16)


def kernel(x_prompt, x_sample, c_prompt, c_sample, cache_k, cache_v, cache_kidx, state_conv,
           rel_bias, ada_w, ada_b, norm_pre, norm_post, w_in, conv_w, conv_b, conv_ln_g,
           conv_ln_b, w_conv_out, w_attn_out, w_out):
    depth = ada_w.shape[0]
    bp, _, d = x_prompt.shape
    d_conv = conv_w.shape[2]
    d_attn = N_HEADS * HEAD_DIM
    mod = _ada(jnp.concatenate([c_prompt, c_sample], axis=0), ada_w, ada_b)
    mod = mod.reshape(depth, mod.shape[1], 3, d)
    tb = _bias_tiles(rel_bias)
    tri = (jnp.arange(KEY_BLOCK)[:, None] >= jnp.arange(KEY_BLOCK)[None, :]).astype(BF16)

    lp, ls = x_prompt.shape[1], x_sample.shape[1]
    xp, xs = x_prompt, _pad_axis(x_sample, 1, -(-ls // LANES) * LANES)
    kvd = N_KV_HEADS * HEAD_DIM

    def new_states(batch, rows):
        return tuple(jnp.zeros((depth, batch, rows, width), F32) for width in (kvd, kvd, IDX_DIM))

    sp, ss = new_states(bp, lp), new_states(xs.shape[0], xs.shape[1])
    conv_p, conv_s = [], []
    for l in range(depth):
        w = dict(
            g_pre=norm_pre[l][None], g_post=norm_post[l][None],
            w_in=_pack_w_in(w_in[l], d_conv, d_attn, d),
            conv_w=conv_w[l], conv_b=conv_b[l][None], ln_g=conv_ln_g[l][None],
            ln_b=conv_ln_b[l][None], w_conv_out=w_conv_out[l].astype(BF16),
            w_attn_out=w_attn_out[l].astype(BF16), w_out=w_out[l].astype(BF16))
        zeros_conv = jnp.zeros((bp, CONV_W - 1, d_conv), xp.dtype)
        xp, cp, sp = _sublayer(xp, lp, mod[l, :bp], zeros_conv, None, tb, tri, w, sp, l)
        xs, cs, ss = _sublayer(xs, ls, mod[l, bp:], state_conv[l],
                               (cache_k[l], cache_v[l], cache_kidx[l]), tb, tri, w, ss, l)
        conv_p.append(cp)
        conv_s.append(cs)

    def unpack(states, rows):
        k_all, v_all, ki_all = (t[:, :, :rows] for t in states)
        heads = k_all.shape[:3] + (N_KV_HEADS, HEAD_DIM)
        return k_all.reshape(heads), v_all.reshape(heads), ki_all

    return ((xp, xs[:, :ls]) + unpack(sp, lp) + (jnp.stack(conv_p),)
            + unpack(ss, ls) + (jnp.stack(conv_s),))
```

```python
import functools
import math

import jax
import jax.numpy as jnp
from jax import lax
from jax.experimental import pallas as pl
from jax.experimental.pallas import tpu as pltpu

F32 = jnp.float32
BF16 = jnp.bfloat16
I32 = jnp.int32
I16 = jnp.int16

CHUNK = 64
CONV_W = 31
N_HEADS = 16
N_KV_HEADS = 4
HEAD_DIM = 64
GROUP = N_HEADS // N_KV_HEADS
IDX_HEADS = 8
IDX_DIM = 64
TOPK_MAX = 256
N_BUCKETS = 32
MAX_DIST = 128
EPS = 1e-6

LANES = 128
SUBLANES = 8
PACKED_ROWS = 16
KEY_BLOCK = 256
BIAS_TILE = 128
CONV_HALO = 32
CONV_ROWS = 128
POST_ROWS = 256
VMEM_LIMIT_BYTES = 56 * 1024 * 1024
NEG_INF = float("-inf")
INT_MIN = -2 ** 31
HALF_MIN = -2 ** 15
LOG2E = 1.4426950408889634


def _sigmoid(x):
    return 1.0 / (1.0 + jnp.exp(-x))


def _silu(x):
    return x * _sigmoid(x)


def _params(*sem):
    return pltpu.CompilerParams(dimension_semantics=sem, vmem_limit_bytes=VMEM_LIMIT_BYTES)


def _ada_kernel(c_ref, w_ref, b_ref, o_ref):
    c = c_ref[...]
    o_ref[0] = jnp.dot(_silu(c), w_ref[0], preferred_element_type=F32) + b_ref[0]


def _ada(c_all, ada_w, ada_b):
    depth, d, _ = ada_w.shape
    bc = c_all.shape[0]
    return pl.pallas_call(
        _ada_kernel,
        grid=(depth, 3),
        in_specs=[
            pl.BlockSpec((bc, d), lambda l, n: (0, 0)),
            pl.BlockSpec((1, d, d), lambda l, n: (l, 0, n)),
            pl.BlockSpec((1, 1, d), lambda l, n: (l, 0, n)),
        ],
        out_specs=pl.BlockSpec((1, bc, d), lambda l, n: (l, 0, n)),
        out_shape=jax.ShapeDtypeStruct((depth, bc, 3 * d), F32),
        compiler_params=_params("arbitrary", "arbitrary"),
        name="ada",
    )(c_all, ada_w, ada_b.reshape(depth, 1, 3 * d))


def _bias_kernel(rb_ref, o_ref):
    r = lax.broadcasted_iota(I32, (BIAS_TILE, BIAS_TILE), 0)
    c = lax.broadcasted_iota(I32, (BIAS_TILE, BIAS_TILE), 1)
    nb = N_BUCKETS // 2
    max_exact = nb // 2
    for t in range(2):
        rel = (t - 1) * BIAS_TILE + r - c
        ret = jnp.where(rel > 0, nb, 0)
        n = jnp.abs(rel)
        nf = jnp.maximum(n, 1).astype(F32)
        large = max_exact + (jnp.log(nf / max_exact) / math.log(MAX_DIST / max_exact)
                             * (nb - max_exact)).astype(I32)
        large = jnp.minimum(large, nb - 1)
        bucket = ret + jnp.where(n < max_exact, n, large)
        for h in range(N_HEADS):
            acc = jnp.zeros((BIAS_TILE, BIAS_TILE), F32)
            for b in range(N_BUCKETS):
                acc = jnp.where(bucket == b, rb_ref[b, h], acc)
            o_ref[h, t] = acc * LOG2E


def _bias_tiles(rel_bias):
    return pl.pallas_call(
        _bias_kernel,
        in_specs=[pl.BlockSpec(memory_space=pltpu.SMEM)],
        out_specs=pl.BlockSpec(memory_space=pltpu.VMEM),
        out_shape=jax.ShapeDtypeStruct((N_HEADS, 2, BIAS_TILE, BIAS_TILE), F32),
        name="bias_tiles",
    )(rel_bias)


def _seg_offsets(d_conv, d_attn):
    widths = (2 * d_conv, d_conv, d_attn, N_KV_HEADS * HEAD_DIM, N_KV_HEADS * HEAD_DIM, d_attn,
              IDX_HEADS * IDX_DIM, LANES, None)
    offs, acc = [], 0
    for w in widths:
        offs.append(acc)
        if w is not None:
            acc += w
    return offs


def _inproj_kernel(x_ref, mod_ref, g_ref, w_ref, *refs, tm, d_conv, d_attn, d_model, n_aliased):
    (u_ref, sgc_ref, qt_ref, k_ref, v_ref, kb_ref, vt_ref, sga_ref, qit_ref,
     ki_ref, kib_ref, wi_ref, gm_ref) = refs[n_aliased:]
    x = x_ref[0]
    ms = jnp.mean(x * x, axis=-1, keepdims=True)
    y = x * lax.rsqrt(ms + EPS) * g_ref[...]
    shift = mod_ref[0, 0:1, :]
    scale = mod_ref[0, 1:2, :]
    h = (y * (1.0 + scale) + shift).astype(BF16)
    o_a, o_gc, o_q, o_k, o_v, o_ga, o_qi, o_kw, o_gm = _seg_offsets(d_conv, d_attn)

    def mm(lo, width):
        return jnp.dot(h, w_ref[:, lo:lo + width], preferred_element_type=F32)

    a = mm(o_a, 2 * d_conv)
    u_ref[0] = a[:, :d_conv] * _sigmoid(a[:, d_conv:])
    sgc_ref[0] = _silu(mm(o_gc, d_conv)).astype(BF16)
    zqt = (mm(o_q, d_attn) * (HEAD_DIM ** -0.5 * LOG2E)).T
    for hh in range(N_HEADS):
        g, j = divmod(hh, GROUP)
        qt_ref[0, g, 0, :, j * tm:(j + 1) * tm] = (
            zqt[hh * HEAD_DIM:(hh + 1) * HEAD_DIM].astype(BF16))
    zk = mm(o_k, N_KV_HEADS * HEAD_DIM)
    zv = mm(o_v, N_KV_HEADS * HEAD_DIM)
    for dl in range(k_ref.shape[0]):
        k_ref[dl, 0] = zk
        v_ref[dl, 0] = zv
    zvt = zv.T
    ones_rows = jnp.where(lax.broadcasted_iota(I32, (PACKED_ROWS, tm), 0) == 0, 1.0, 0.0)
    for g in range(N_KV_HEADS):
        kb_ref[0, g] = zk[:, g * HEAD_DIM:(g + 1) * HEAD_DIM].astype(BF16)
        vt_ref[0, g, 0, 0:HEAD_DIM] = zvt[g * HEAD_DIM:(g + 1) * HEAD_DIM].astype(BF16)
        vt_ref[0, g, 0, HEAD_DIM:] = ones_rows.astype(BF16)
    sga_ref[0] = _silu(mm(o_ga, d_attn)).astype(BF16)
    zqit = (mm(o_qi, IDX_HEADS * IDX_DIM) * (IDX_DIM ** -0.5)).T
    for hh in range(IDX_HEADS):
        qit_ref[0, 0, :, hh * tm:(hh + 1) * tm] = zqit[hh * IDX_DIM:(hh + 1) * IDX_DIM].astype(BF16)
    zkw = mm(o_kw, LANES)
    zki = zkw[:, :IDX_DIM]
    for dl in range(ki_ref.shape[0]):
        ki_ref[dl, 0] = zki
    kib_ref[0] = zki.astype(BF16)
    wi_ref[0] = zkw.T[IDX_DIM:IDX_DIM + IDX_HEADS] * (IDX_HEADS ** -0.5)
    gm_ref[0] = _sigmoid(mm(o_gm, 2 * d_model)).astype(BF16)


def _inproj(x, mod3, g_pre, w_pad, states, layer, depth, *, d_conv, d_attn):
    b, l, d = x.shape
    tm = min(KEY_BLOCK, l)
    nw = w_pad.shape[1]
    kvd = N_KV_HEADS * HEAD_DIM

    def row(width):
        return pl.BlockSpec((1, tm, width), lambda bi, i: (bi, i, 0))

    def heads(n, width):
        return pl.BlockSpec((1, n, tm, width), lambda bi, i: (bi, 0, i, 0))

    def layer_rows(width):
        if states is None:
            return pl.BlockSpec((depth, 1, tm, width), lambda bi, i: (0, bi, i, 0))
        return pl.BlockSpec((1, 1, tm, width), lambda bi, i: (layer, bi, i, 0))

    nq = l // tm
    vrows = HEAD_DIM + PACKED_ROWS
    aliased = () if states is None else tuple(states)
    k_all, v_all, ki_all = (jax.ShapeDtypeStruct((depth, b, l, width), F32)
                            for width in (kvd, kvd, IDX_DIM))
    out_shape = (
        jax.ShapeDtypeStruct((b, l, d_conv), F32),
        jax.ShapeDtypeStruct((b, l, d_conv), BF16),
        jax.ShapeDtypeStruct((b, N_KV_HEADS, nq, HEAD_DIM, GROUP * tm), BF16),
        jax.ShapeDtypeStruct(k_all.shape, F32),
        jax.ShapeDtypeStruct(v_all.shape, F32),
        jax.ShapeDtypeStruct((b, N_KV_HEADS, l, HEAD_DIM), BF16),
        jax.ShapeDtypeStruct((b, N_KV_HEADS, nq, vrows, tm), BF16),
        jax.ShapeDtypeStruct((b, l, d_attn), BF16),
        jax.ShapeDtypeStruct((b, nq, IDX_DIM, IDX_HEADS * tm), BF16),
        jax.ShapeDtypeStruct(ki_all.shape, F32),
        jax.ShapeDtypeStruct((b, l, IDX_DIM), BF16),
        jax.ShapeDtypeStruct((b, IDX_HEADS, l), F32),
        jax.ShapeDtypeStruct((b, l, 2 * d), BF16),
    )
    out_specs = (row(d_conv), row(d_conv),
                 pl.BlockSpec((1, N_KV_HEADS, 1, HEAD_DIM, GROUP * tm), lambda bi, i: (bi, 0, i, 0, 0)),
                 layer_rows(kvd), layer_rows(kvd), heads(N_KV_HEADS, HEAD_DIM),
                 pl.BlockSpec((1, N_KV_HEADS, 1, vrows, tm), lambda bi, i: (bi, 0, i, 0, 0)),
                 row(d_attn),
                 pl.BlockSpec((1, 1, IDX_DIM, IDX_HEADS * tm), lambda bi, i: (bi, i, 0, 0)),
                 layer_rows(IDX_DIM), row(IDX_DIM),
                 pl.BlockSpec((1, IDX_HEADS, tm), lambda bi, i: (bi, 0, i)),
                 row(2 * d))
    return pl.pallas_call(
        functools.partial(_inproj_kernel, tm=tm, d_conv=d_conv, d_attn=d_attn, d_model=d,
                          n_aliased=len(aliased)),
        grid=(b, l // tm),
        in_specs=[
            pl.BlockSpec((1, tm, d), lambda bi, i: (bi, i, 0)),
            pl.BlockSpec((1, 3, d), lambda bi, i: (bi, 0, 0)),
            pl.BlockSpec((1, d), lambda bi, i: (0, 0)),
            pl.BlockSpec((d, nw), lambda bi, i: (0, 0), pipeline_mode=pl.Buffered(1)),
        ] + [pl.BlockSpec(memory_space=pl.ANY) for _ in aliased],
        out_specs=out_specs,
        out_shape=out_shape,
        input_output_aliases={4: 3, 5: 4, 6: 9} if aliased else {},
        compiler_params=_params("arbitrary", "arbitrary"),
        name="inproj",
    )(x, mod3, g_pre, w_pad, *aliased)


def _attn_kernel(qt_ref, qit_ref, wi_ref, k_ref, vt_ref, ki_ref, tb_ref, tri_ref, o_ref,
                 key_scr, hi_scr, lo_scr, mb_scr, m_scr, acc_scr, ta_scr, sa_scr, tb_scr, sb_scr,
                 bias_scr, *, tq, past, ksel):
    i = pl.program_id(1)
    q0 = past + i * tq
    nkb = (q0 + tq + KEY_BLOCK - 1) // KEY_BLOCK

    def key_rows(kb):
        return pl.ds(pl.multiple_of(kb * KEY_BLOCK, KEY_BLOCK), KEY_BLOCK)

    qit = qit_ref[0, 0]
    wi = wi_ref[0]
    last = nkb - 1
    last_ok = (lax.broadcasted_iota(I32, (KEY_BLOCK, tq), 0) // CHUNK
               <= lax.broadcasted_iota(I32, (KEY_BLOCK, tq), 1) // CHUNK)

    def score_block(kb, is_last):
        ki_blk = ki_ref[0, key_rows(kb), :]
        acc = jnp.zeros((KEY_BLOCK, tq), F32)
        for h in range(IDX_HEADS):
            s = jnp.dot(ki_blk, qit[:, h * tq:(h + 1) * tq], preferred_element_type=F32)
            acc = acc + jnp.maximum(s, 0.0) * wi[h:h + 1, :]
        sc = jnp.where(last_ok, acc, NEG_INF) if is_last else acc
        bits = pltpu.bitcast(sc, I32)
        key = bits ^ ((bits >> 31) & 0x7FFFFFFF)
        key_scr[kb] = key
        half = pl.ds(pl.multiple_of((kb % 2) * KEY_BLOCK, KEY_BLOCK), KEY_BLOCK)
        hi_scr[kb // 2, half, :] = (key >> 16).astype(I16)
        lo_scr[kb // 2, half, :] = ((key & 0xFFFF) + HALF_MIN).astype(I16)

    def over_blocks(fn, init=0):
        carry = lax.fori_loop(0, last, lambda kb, c: fn(kb, False, c), init)
        return fn(last, True, carry)

    def score_pair(kp, carry):
        score_block(2 * kp, False)
        score_block(2 * kp + 1, False)
        return carry

    lax.fori_loop(0, last // 2, score_pair, 0)

    @pl.when(last % 2 == 1)
    def _():
        score_block(last - 1, False)

    score_block(last, True)

    npair = (nkb + 1) // 2

    @pl.when(nkb % 2 == 1)
    def _():
        floor16 = jnp.full((KEY_BLOCK, tq), HALF_MIN, I16)
        hi_scr[npair - 1, KEY_BLOCK:, :] = floor16
        lo_scr[npair - 1, KEY_BLOCK:, :] = floor16

    def count16(scr, pred, visit=None):
        groups = 2 * KEY_BLOCK // PACKED_ROWS

        def body(kp, acc):
            x = scr[kp]
            if visit is not None:
                visit(kp, x)
            hit = jnp.where(pred(x), jnp.int16(1), jnp.int16(0))
            hit = hit.reshape(groups, PACKED_ROWS, tq)
            parts = [hit[r] for r in range(groups)]
            while len(parts) > 1:
                parts = [parts[r] + parts[r + 1] for r in range(0, len(parts), 2)]
            return acc + parts[0]
        acc = lax.fori_loop(0, npair, body, jnp.zeros((PACKED_ROWS, tq), I16))
        return acc.astype(I32).sum(axis=0, keepdims=True)

    def search16(scr, target):
        def refine(it, state):
            thr, n_at = state
            cand = thr + jnp.left_shift(jnp.int32(1), 15 - it)
            c16 = cand.astype(I16)
            n = count16(scr, lambda x: x >= c16)
            return jnp.where(n >= target, cand, thr), jnp.where(n >= target, n, n_at)
        every = jnp.full((1, tq), 2 * KEY_BLOCK, I32) * npair
        return lax.fori_loop(0, 16, refine, (jnp.full((1, tq), HALF_MIN, I32), every))

    thr_hi, _ = search16(hi_scr, ksel)
    thr_hi16 = thr_hi.astype(I16)

    def restrict(kp, hi):
        lo_scr[kp] = jnp.where(hi == thr_hi16, lo_scr[kp], jnp.int16(HALF_MIN))

    above = count16(hi_scr, lambda x: x > thr_hi16, visit=restrict)
    thr_lo, n_lo = search16(lo_scr, ksel - above)
    thr = thr_hi * 65536 + (thr_lo - HALF_MIN)
    surplus = jnp.max(above + n_lo) > ksel

    @pl.when(jnp.logical_not(surplus))
    def _():
        def mask_block(kb, is_last, carry):
            keep = jnp.where(key_scr[kb] >= thr, 0.0, NEG_INF)
            mb_scr[kb] = (jnp.where(last_ok, keep, NEG_INF) if is_last else keep).astype(BF16)
            return carry

        over_blocks(mask_block)

    @pl.when(surplus)
    def _():
        def count_gt(kb, acc):
            hit = (key_scr[kb] > thr).astype(I32)
            return acc + hit.reshape(KEY_BLOCK // 8, 8, tq).sum(axis=0)

        n_gt = lax.fori_loop(0, nkb, count_gt, jnp.zeros((8, tq), I32)).sum(axis=0, keepdims=True)
        need = (ksel - n_gt).astype(F32)

        def mask_block(kb, is_last, ties_before):
            kk = key_scr[kb]
            eq = kk == thr
            rank = jnp.dot(tri_ref[...], jnp.where(eq, 1.0, 0.0).astype(BF16),
                           preferred_element_type=F32) + ties_before
            tie = jnp.where(eq, jnp.where(rank <= need, 0.0, NEG_INF), NEG_INF)
            keep = jnp.where(kk > thr, 0.0, tie)
            mb_scr[kb] = (jnp.where(last_ok, keep, NEG_INF) if is_last else keep).astype(BF16)
            return rank[KEY_BLOCK - 1:KEY_BLOCK, :]

        over_blocks(mask_block, jnp.zeros((1, tq), F32))

    nsub = tq // BIAS_TILE

    def bias_block(h, diag):
        same = tb_ref[h, 1]
        prev = tb_ref[h, 0]
        far = jnp.broadcast_to(tb_ref[h, 0, 0:1, 0:1], (BIAS_TILE, BIAS_TILE))
        if diag:
            grid = [[same, prev], [far, same]]
        else:
            grid = [[far, far], [prev, far]]
        return jnp.concatenate([jnp.concatenate(r[:nsub], axis=1) for r in grid], axis=0)

    @pl.when(jnp.logical_and(pl.program_id(0) == 0, i == 0))
    def _():
        for h in range(N_HEADS):
            bias_scr[0, h] = bias_block(h, False).astype(BF16)
            bias_scr[1, h] = bias_block(h, True).astype(BF16)

    def produce(h, kb, kind, buf, gated=False):
        t_buf, s_buf = buf
        g, j = divmod(h, GROUP)
        lanes = slice(j * tq, (j + 1) * tq)
        t = jnp.dot(k_ref[0, g, key_rows(kb), :], qt_ref[0, g, 0, :, lanes],
                    preferred_element_type=F32).astype(BF16) + mb_scr[kb]
        if kind == "far":
            shift = jnp.broadcast_to(tb_ref[h, 0, 0:1, 0:1], (1, tq))
        else:
            shift = jnp.zeros((1, tq), F32)
            t = t + bias_scr[int(kind == "diag"), h]
            if gated:
                t = t + jnp.where(nkb >= 2, 0.0, NEG_INF).astype(BF16)
        t_buf[g, :, lanes] = t
        s_buf[g, 0:1, lanes] = jnp.max(t, axis=0, keepdims=True).astype(F32) + shift
        s_buf[g, 1:2, lanes] = shift

    def consume(h, kb, buf):
        t_buf, s_buf = buf
        g, j = divmod(h, GROUP)
        lanes = slice(j * tq, (j + 1) * tq)
        m = m_scr[g, :, lanes]
        shift = s_buf[g, 1:2, lanes]
        m_new = jnp.maximum(m, s_buf[g, 0:1, lanes])
        sub = (jnp.where(m_new == NEG_INF, 0.0, m_new) - shift).astype(BF16)
        m_used = sub.astype(F32) + shift
        p = jnp.exp2(t_buf[g, :, lanes] - sub)
        acc_scr[g, :, lanes] = jnp.exp2(m - m_used) * acc_scr[g, :, lanes] + jnp.dot(
            vt_ref[0, g, kb], p, preferred_element_type=F32)
        m_scr[g, :, lanes] = jnp.where(m_new == NEG_INF, NEG_INF, m_used)

    def stage(new=None, old=None):
        for h in range(N_HEADS):
            if new is not None:
                produce(h, *new)
            if old is not None:
                consume(h, *old)

    m_scr[...] = jnp.full(m_scr.shape, NEG_INF, F32)
    acc_scr[...] = jnp.zeros(acc_scr.shape, F32)

    buf_a, buf_b = (ta_scr, sa_scr), (tb_scr, sb_scr)
    n_far = jnp.maximum(nkb - 2, 0)
    off = jnp.maximum(nkb - 2, 0)
    stage(new=(last, "diag", buf_a))
    stage(new=(off, "off", buf_b, True), old=(last, buf_a))

    def far_pair(it, carry):
        f = nkb - 3 - 2 * it
        stage(new=(f, "far", buf_a), old=(f + 1, buf_b))
        stage(new=(f - 1, "far", buf_b), old=(f, buf_a))
        return carry

    lax.fori_loop(0, n_far // 2, far_pair, 0)

    @pl.when(n_far % 2 == 1)
    def _():
        stage(new=(0, "far", buf_a), old=(1, buf_b))
        stage(old=(0, buf_a))

    @pl.when(n_far % 2 == 0)
    def _():
        stage(old=(jnp.maximum(nkb - 2 - n_far, 0), buf_b))

    heads_t = []
    for g in range(N_KV_HEADS):
        acc = acc_scr[g]
        o = acc[:HEAD_DIM] / acc[HEAD_DIM:HEAD_DIM + 1]
        heads_t += [o[:, j * tq:(j + 1) * tq] for j in range(GROUP)]
    o_ref[0] = jnp.concatenate(heads_t, axis=0).T.astype(o_ref.dtype)


def _attention(qt, qit, wi_t, k_hm, vt, kib, tb, tri, *, tq, past, ksel):
    b, _, nq, _, _ = qt.shape
    lk = k_hm.shape[2]
    nkb_max = lk // KEY_BLOCK
    blocks = (nkb_max, KEY_BLOCK, tq)
    pairs = ((nkb_max + 1) // 2, 2 * KEY_BLOCK, tq)
    return pl.pallas_call(
        functools.partial(_attn_kernel, tq=tq, past=past, ksel=ksel),
        grid=(b, nq),
        in_specs=[
            pl.BlockSpec((1, N_KV_HEADS, 1, HEAD_DIM, GROUP * tq), lambda bi, i: (bi, 0, i, 0, 0)),
            pl.BlockSpec((1, 1, IDX_DIM, IDX_HEADS * tq), lambda bi, i: (bi, i, 0, 0)),
            pl.BlockSpec((1, IDX_HEADS, tq), lambda bi, i: (bi, 0, i)),
            pl.BlockSpec((1, N_KV_HEADS, lk, HEAD_DIM), lambda bi, i: (bi, 0, 0, 0)),
            pl.BlockSpec((1, N_KV_HEADS, nkb_max, vt.shape[3], KEY_BLOCK),
                         lambda bi, i: (bi, 0, 0, 0, 0)),
            pl.BlockSpec((1, lk, IDX_DIM), lambda bi, i: (bi, 0, 0)),
            pl.BlockSpec((N_HEADS, 2, BIAS_TILE, BIAS_TILE), lambda bi, i: (0, 0, 0, 0)),
            pl.BlockSpec((KEY_BLOCK, KEY_BLOCK), lambda bi, i: (0, 0)),
        ],
        out_specs=pl.BlockSpec((1, tq, N_HEADS * HEAD_DIM), lambda bi, i: (bi, i, 0)),
        out_shape=jax.ShapeDtypeStruct((b, nq * tq, N_HEADS * HEAD_DIM), BF16),
        scratch_shapes=[pltpu.VMEM(blocks, I32), pltpu.VMEM(pairs, I16), pltpu.VMEM(pairs, I16),
                        pltpu.VMEM(blocks, BF16),
                        pltpu.VMEM((N_KV_HEADS, 1, GROUP * tq), F32),
                        pltpu.VMEM((N_KV_HEADS, vt.shape[3], GROUP * tq), F32)]
        + 2 * [pltpu.VMEM((N_KV_HEADS, KEY_BLOCK, GROUP * tq), BF16),
               pltpu.VMEM((N_KV_HEADS, 2, GROUP * tq), F32)]
        + [pltpu.VMEM((2, N_HEADS, KEY_BLOCK, tq), BF16)],
        compiler_params=_params("arbitrary", "arbitrary"),
        name="attention",
    )(qt, qit, wi_t, k_hm, vt, kib, tb, tri)


def _post_kernel(x_ref, mod_ref, u_ref, halo_ref, prev_ref, sgc_ref, o_ref, sga_ref, gm_ref,
                 cw_ref, cb_ref, lg_ref, lb_ref, wc_ref, wa_ref, wo_ref, gp_ref,
                 y_ref, up_scr, sh_scr, cv_scr, *, tt, d_conv, d_model):
    i = pl.program_id(1)
    up_scr[0:CONV_HALO] = jnp.where(i == 0, prev_ref[0], halo_ref[0])
    up_scr[CONV_HALO:] = u_ref[0]
    span = tt + CONV_HALO - SUBLANES
    for s in range(1, SUBLANES):
        sh_scr[s - 1] = up_scr[pl.ds(s, span), :]
    pad = CONV_HALO - (CONV_W - 1)
    rt = min(tt, CONV_ROWS)
    for r0 in range(0, tt, rt):
        for c0 in range(0, d_conv, LANES):
            acc = jnp.broadcast_to(cb_ref[:, c0:c0 + LANES], (rt, LANES))
            for j in range(CONV_W):
                a, s = divmod(pad + j, SUBLANES)
                rows = pl.ds(r0 + SUBLANES * a, rt)
                win = up_scr[rows, c0:c0 + LANES] if s == 0 else sh_scr[s - 1, rows, c0:c0 + LANES]
                acc = acc + cw_ref[j:j + 1, c0:c0 + LANES] * win
            cv_scr[r0:r0 + rt, c0:c0 + LANES] = acc
    cv = cv_scr[...]
    mu = jnp.mean(cv, axis=-1, keepdims=True)
    dv = cv - mu
    var = jnp.mean(dv * dv, axis=-1, keepdims=True)
    yc = _silu(dv * lax.rsqrt(var + EPS) * lg_ref[...] + lb_ref[...])
    yc = (yc * sgc_ref[0].astype(F32)).astype(BF16)
    y_c = jnp.dot(yc, wc_ref[...], preferred_element_type=F32)
    ya = (o_ref[0].astype(F32) * sga_ref[0].astype(F32)).astype(BF16)
    y_a = jnp.dot(ya, wa_ref[...], preferred_element_type=F32)
    gm = gm_ref[0].astype(F32)
    mix = (gm[:, :d_model] * y_c + gm[:, d_model:] * y_a).astype(BF16)
    y = jnp.dot(mix, wo_ref[...], preferred_element_type=F32)
    y = y * lax.rsqrt(jnp.mean(y * y, axis=-1, keepdims=True) + EPS) * gp_ref[...]
    y_ref[0] = x_ref[0] + mod_ref[0, 2:3, :] * y


def _post(x, mod3, u, prev, sgc, o, sga, gm, conv_w, conv_b, ln_g, ln_b, wc, wa, wo, g_post):
    b, l, d = x.shape
    d_conv = u.shape[2]
    d_attn = o.shape[2]
    tt = min(POST_ROWS, l)
    hb = tt // CONV_HALO

    def row(width):
        return pl.BlockSpec((1, tt, width), lambda bi, i: (bi, i, 0))

    def const(shape):
        return pl.BlockSpec(shape, lambda bi, i: (0,) * len(shape))

    return pl.pallas_call(
        functools.partial(_post_kernel, tt=tt, d_conv=d_conv, d_model=d),
        grid=(b, l // tt),
        in_specs=[
            row(d),
            pl.BlockSpec((1, 3, d), lambda bi, i: (bi, 0, 0)),
            row(d_conv),
            pl.BlockSpec((1, CONV_HALO, d_conv), lambda bi, i: (bi, jnp.maximum(i * hb - 1, 0), 0)),
            pl.BlockSpec((1, CONV_HALO, d_conv), lambda bi, i: (bi, 0, 0)),
            row(d_conv), row(d_attn), row(d_attn), row(2 * d),
            const((CONV_W, d_conv)), const((1, d_conv)), const((1, d_conv)), const((1, d_conv)),
            const((d_conv, d)), const((d_attn, d)), const((d, d)), const((1, d)),
        ],
        out_specs=row(d),
        out_shape=jax.ShapeDtypeStruct((b, l, d), F32),
        scratch_shapes=[pltpu.VMEM((tt + CONV_HALO, d_conv), F32),
                        pltpu.VMEM((SUBLANES - 1, tt + CONV_HALO - SUBLANES, d_conv), F32),
                        pltpu.VMEM((tt, d_conv), F32)],
        compiler_params=_params("arbitrary", "arbitrary"),
        name="post",
    )(x, mod3, u, u, prev, sgc, o, sga, gm, conv_w, conv_b, ln_g, ln_b, wc, wa, wo, g_post)


def _pad_axis(t, axis, size):
    if t.shape[axis] == size:
        return t
    widths = [(0, 0)] * t.ndim
    widths[axis] = (0, size - t.shape[axis])
    return jnp.pad(t, widths)


def _sublayer(x, l_true, mod3, conv_prev, past, tb, tri, w, states, layer, depth):
    b, l, d = x.shape
    d_conv = w["conv_w"].shape[1]
    d_attn = N_HEADS * HEAD_DIM
    (u, sgc, qt, k_all, v_all, k_hm, vt, sga, qit, ki_all, kib, wi_t, gm) = _inproj(
        x, mod3, w["g_pre"], w["w_in"], states, layer, depth, d_conv=d_conv, d_attn=d_attn)

    if past is None:
        p_len = 0
    else:
        ck, cv, cki = past
        p_len = ck.shape[1]
        k_hm = jnp.concatenate([ck.transpose(0, 2, 1, 3).astype(BF16), k_hm], axis=2)
        ones_rows = jnp.zeros((b, N_KV_HEADS, PACKED_ROWS, p_len), BF16).at[:, :, 0].set(1.0)
        cvt = jnp.concatenate([cv.transpose(0, 2, 3, 1).astype(BF16), ones_rows], axis=2)
        cvt = cvt.reshape(b, N_KV_HEADS, cvt.shape[2], p_len // KEY_BLOCK, KEY_BLOCK)
        vt = jnp.concatenate([cvt.transpose(0, 1, 3, 2, 4), _pad_axis(vt, 4, KEY_BLOCK)], axis=2)
        kib = jnp.concatenate([cki.astype(BF16), kib], axis=1)
    ksel = min(TOPK_MAX, (p_len + l_true) // 4)
    tq = min(KEY_BLOCK, l)
    lk_pad = -(-(p_len + l) // KEY_BLOCK) * KEY_BLOCK
    assert p_len % KEY_BLOCK == 0 and (tq == KEY_BLOCK or l == tq)
    assert (p_len + l_true) % CHUNK == 0 and ksel <= KEY_BLOCK
    o = _attention(qt, qit, wi_t, _pad_axis(k_hm, 2, lk_pad), vt, _pad_axis(kib, 1, lk_pad),
                   tb, tri, tq=tq, past=p_len, ksel=ksel)

    prev = jnp.pad(conv_prev, ((0, 0), (CONV_HALO - (CONV_W - 1), 0), (0, 0)))
    y = _post(x, mod3, u, prev, sgc, o, sga, gm, w["conv_w"], w["conv_b"], w["ln_g"], w["ln_b"],
              w["w_conv_out"], w["w_attn_out"], w["w_out"], w["g_post"])
    conv_new = jnp.concatenate([conv_prev, u[:, :l_true]], axis=1)[:, -(CONV_W - 1):]
    return y, conv_new, (k_all, v_all, ki_all)


def _pack_w_in(w_in, d_conv, d_attn, d_model):
    kvd = N_KV_HEADS * HEAD_DIM
    splits = (2 * d_conv, d_conv, d_attn, kvd, kvd, d_attn, IDX_HEADS * IDX_DIM, IDX_DIM,
              IDX_HEADS, 2 * d_model)
    offs = [0]
    for s in splits:
        offs.append(offs[-1] + s)
    seg = [w_in[:, offs[j]:offs[j + 1]] for j in range(len(splits))]
    kw = jnp.concatenate([seg[7], seg[8]], axis=1)
    kw = jnp.pad(kw, ((0, 0), (0, LANES - kw.shape[1])))
    return jnp.concatenate(seg[:7] + [kw, seg[9]], axis=1).astype(BF16)


def kernel(x_prompt, x_sample, c_prompt, c_sample, cache_k, cache_v, cache_kidx, state_conv,
           rel_bias, ada_w, ada_b, norm_pre, norm_post, w_in, conv_w, conv_b, conv_ln_g,
           conv_ln_b, w_conv_out, w_attn_out, w_out):
    depth = ada_w.shape[0]
    bp, _, d = x_prompt.shape
    d_conv = conv_w.shape[2]
    d_attn = N_HEADS * HEAD_DIM
    mod = _ada(jnp.concatenate([c_prompt, c_sample], axis=0), ada_w, ada_b)
    mod = mod.reshape(depth, mod.shape[1], 3, d)
    tb = _bias_tiles(rel_bias)
    tri = (jnp.arange(KEY_BLOCK)[:, None] >= jnp.arange(KEY_BLOCK)[None, :]).astype(BF16)

    lp, ls = x_prompt.shape[1], x_sample.shape[1]
    xp, xs = x_prompt, _pad_axis(x_sample, 1, -(-ls // LANES) * LANES)
    sp = ss = None
    conv_p, conv_s = [], []
    for l in range(depth):
        w = dict(
            g_pre=norm_pre[l][None], g_post=norm_post[l][None],
            w_in=_pack_w_in(w_in[l], d_conv, d_attn, d),
            conv_w=conv_w[l], conv_b=conv_b[l][None], ln_g=conv_ln_g[l][None],
            ln_b=conv_ln_b[l][None], w_conv_out=w_conv_out[l].astype(BF16),
            w_attn_out=w_attn_out[l].astype(BF16), w_out=w_out[l].astype(BF16))
        zeros_conv = jnp.zeros((bp, CONV_W - 1, d_conv), xp.dtype)
        xp, cp, sp = _sublayer(xp, lp, mod[l, :bp], zeros_conv, None, tb, tri, w, sp, l, depth)
        xs, cs, ss = _sublayer(xs, ls, mod[l, bp:], state_conv[l],
                               (cache_k[l], cache_v[l], cache_kidx[l]), tb, tri, w, ss, l, depth)
        conv_p.append(cp)
        conv_s.append(cs)

    def unpack(states, rows):
        k_all, v_all, ki_all = (t[:, :, :rows] for t in states)
        heads = k_all.shape[:3] + (N_KV_HEADS, HEAD_DIM)
        return k_all.reshape(heads), v_all.reshape(heads), ki_all

    return ((xp, xs[:, :ls]) + unpack(sp, lp) + (jnp.stack(conv_p),)
            + unpack(ss, ls) + (jnp.stack(conv_s),))
```
